```python
import jax, jax.numpy as jnp
from jax import lax
import numpy as np

D_MODEL = 1024
BATCH = 4
SEQ = 8192
DEPTH = 4

CHUNK = 64
N_MIXERS = 2
POOL_WINDOWS = (2, 4, 8, 16)
POOL_GROUPS = len(POOL_WINDOWS)
POOL_GW = D_MODEL // POOL_GROUPS
N_HEADS = 16
HEAD_DIM = D_MODEL // N_HEADS
LEFT_CHUNKS = 8
BAND_CHUNKS = LEFT_CHUNKS + 1
BAND = BAND_CHUNKS * CHUNK
REL_CLIP = 256
D_FF = 2816
CONV_W = 3
N_A = (DEPTH + 1) // 2
N_B = DEPTH // 2
EPS = 1e-6

kernel_name = "hybrid_pool_chunkattn_convffn"


def rmsnorm(x, g):
    xf = x.astype(jnp.float32)
    y = xf * lax.rsqrt(jnp.mean(xf * xf, axis=-1, keepdims=True) + EPS)
    return (y * g.astype(jnp.float32)).astype(x.dtype)


def pool_mixer(h, w_pool, b_pool, scale):
    B, S, D = h.shape
    hf = h.astype(jnp.float32)
    csp = jnp.concatenate([jnp.zeros((B, 1, D), jnp.float32), jnp.cumsum(hf, axis=1)], axis=1)
    t = jnp.arange(S)
    outs = []
    for g, w in enumerate(POOL_WINDOWS):
        c = csp[:, :, g * POOL_GW:(g + 1) * POOL_GW]
        upper = c[:, 1:]
        lower = jnp.concatenate([jnp.zeros((B, w - 1, POOL_GW), jnp.float32), c[:, :S - w + 1]], axis=1)
        cnt = jnp.minimum(t + 1, w).astype(jnp.float32)[None, :, None]
        outs.append((upper - lower) / cnt)
    pooled = jnp.concatenate(outs, axis=-1)
    y = (pooled - hf).astype(h.dtype).reshape(B, S, POOL_GROUPS, POOL_GW)
    y = jnp.einsum('bsgc,gcd->bsgd', y, w_pool).reshape(B, S, D) + b_pool
    return y * scale


def head_rmsnorm(x, g):
    xf = x.astype(jnp.float32)
    y = xf * lax.rsqrt(jnp.mean(xf * xf, axis=-1, keepdims=True) + EPS)
    return (y * g.astype(jnp.float32)).astype(x.dtype)


def chunk_attention(h, w_qkv, q_norm, k_norm, rel_table, w_o):
    B, S, D = h.shape
    nc = S // CHUNK
    qkv = h @ w_qkv
    q, k, v = jnp.split(qkv, 3, axis=-1)
    q = head_rmsnorm(q.reshape(B, S, N_HEADS, HEAD_DIM), q_norm)
    k = head_rmsnorm(k.reshape(B, S, N_HEADS, HEAD_DIM), k_norm)
    v = v.reshape(B, S, N_HEADS, HEAD_DIM)
    q = q.reshape(B, nc, CHUNK, N_HEADS, HEAD_DIM)
    pad = jnp.zeros((B, LEFT_CHUNKS, CHUNK, N_HEADS, HEAD_DIM), k.dtype)
    kp = jnp.concatenate([pad, k.reshape(B, nc, CHUNK, N_HEADS, HEAD_DIM)], axis=1)
    vp = jnp.concatenate([pad, v.reshape(B, nc, CHUNK, N_HEADS, HEAD_DIM)], axis=1)
    rel = jnp.arange(CHUNK)[:, None] - jnp.arange(BAND)[None, :] + LEFT_CHUNKS * CHUNK
    idx = jnp.clip(rel, -REL_CLIP, REL_CLIP) + REL_CLIP
    bias = rel_table.astype(jnp.float32)[:, idx]
    band_chunk = jnp.repeat(jnp.arange(BAND_CHUNKS), CHUNK)
    scale = HEAD_DIM ** -0.5

    def one_chunk(c):
        qc = lax.dynamic_index_in_dim(q, c, axis=1, keepdims=False)
        kb = lax.dynamic_slice_in_dim(kp, c, BAND_CHUNKS, axis=1).reshape(B, BAND, N_HEADS, HEAD_DIM)
        vb = lax.dynamic_slice_in_dim(vp, c, BAND_CHUNKS, axis=1).reshape(B, BAND, N_HEADS, HEAD_DIM)
        s = jnp.einsum('bqhd,bkhd->bhqk', qc, kb).astype(jnp.float32) * scale + bias[None]
        valid = (band_chunk + c - LEFT_CHUNKS) >= 0
        s = jnp.where(valid[None, None, None, :], s, -1e30)
        p = jax.nn.softmax(s, axis=-1).astype(vb.dtype)
        return jnp.einsum('bhqk,bkhd->bqhd', p, vb)

    o = lax.map(one_chunk, jnp.arange(nc))
    o = jnp.transpose(o, (1, 0, 2, 3, 4)).reshape(B, S, D)
    return o @ w_o


def conv_ffn(h, w_gate, w_val, conv_w, conv_b, w_out):
    S = h.shape[1]
    a = h @ w_gate
    ap = jnp.pad(a, ((0, 0), (CONV_W - 1, 0), (0, 0)))
    a = ap[:, 0:S] * conv_w[0] + ap[:, 1:S + 1] * conv_w[1] + ap[:, 2:S + 2] * conv_w[2] + conv_b
    return (jax.nn.silu(a) * (h @ w_val)) @ w_out


def setup_inputs(seed: int = 0) -> dict:
    key = jax.random.key(seed)
    ks = jax.random.split(key, 16)
    f32 = jnp.float32
    D, F = D_MODEL, D_FF
    nrm = lambda k, shape, s: jax.random.normal(k, shape, f32) * s
    return {
        "x": nrm(ks[0], (BATCH, SEQ, D), 1.0),
        "mix_norm": 1.0 + nrm(ks[1], (DEPTH, D), 0.02),
        "ffn_norm": 1.0 + nrm(ks[2], (DEPTH, D), 0.02),
        "pool_w": nrm(ks[3], (N_A, POOL_GROUPS, POOL_GW, POOL_GW), POOL_GW ** -0.5),
        "pool_b": nrm(ks[4], (N_A, D), 0.02),
        "pool_scale": 1.0 + nrm(ks[5], (N_A, D), 0.1),
        "attn_wqkv": nrm(ks[6], (N_B, D, 3 * D), D ** -0.5),
        "attn_q_norm": 1.0 + nrm(ks[7], (N_B, HEAD_DIM), 0.02),
        "attn_k_norm": 1.0 + nrm(ks[8], (N_B, HEAD_DIM), 0.02),
        "attn_rel_bias": nrm(ks[9], (N_B, N_HEADS, 2 * REL_CLIP + 1), 0.1),
        "attn_wo": nrm(ks[10], (N_B, D, D), D ** -0.5),
        "ffn_w_gate": nrm(ks[11], (DEPTH, D, F), D ** -0.5),
        "ffn_w_val": nrm(ks[12], (DEPTH, D, F), D ** -0.5),
        "ffn_conv_w": nrm(ks[13], (DEPTH, CONV_W, F), CONV_W ** -0.5),
        "ffn_conv_b": nrm(ks[14], (DEPTH, F), 0.02),
        "ffn_w_out": nrm(ks[15], (DEPTH, F, D), F ** -0.5),
    }


def reference(x, mix_norm, ffn_norm, pool_w, pool_b, pool_scale, attn_wqkv, attn_q_norm,
              attn_k_norm, attn_rel_bias, attn_wo, ffn_w_gate, ffn_w_val, ffn_conv_w,
              ffn_conv_b, ffn_w_out):
    for i in range(DEPTH):
        h = rmsnorm(x, mix_norm[i])
        j = i // N_MIXERS
        if i % N_MIXERS == 0:
            x = x + pool_mixer(h, pool_w[j], pool_b[j], pool_scale[j])
        else:
            x = x + chunk_attention(h, attn_wqkv[j], attn_q_norm[j], attn_k_norm[j],
                                    attn_rel_bias[j], attn_wo[j])
        h = rmsnorm(x, ffn_norm[i])
        x = x + conv_ffn(h, ffn_w_gate[i], ffn_w_val[i], ffn_conv_w[i], ffn_conv_b[i], ffn_w_out[i])
    return x
```

```python
import functools

import jax
import jax.numpy as jnp
from jax import lax
from jax.experimental import pallas as pl
from jax.experimental.pallas import tpu as pltpu

EPS = 1e-6
CHUNK = 64
LEFT_CHUNKS = 8
N_HEADS = 16
HEAD_DIM = 64
REL_CLIP = 256
POOL_WINDOWS = (2, 4, 8, 16)
CONV_W = 3
NEG_BIG = -1e30

SUBLANES = 8
LANES = 128
MXU_COLS = 256

ROW_TILE = 512
POOL_HALO = 16
ATTN_SUB = 128
ATTN_WIN = ATTN_SUB + LEFT_CHUNKS * CHUNK
FFN_COLS = 256
VMEM_LIMIT = 56 * 1024 * 1024

_BF16 = jnp.bfloat16
_F32 = jnp.float32


def _rmsnorm(x, g):
    ms = jnp.mean(x * x, axis=-1, keepdims=True)
    return x * lax.rsqrt(ms + EPS) * g


def _dot(a, b):
    return jnp.dot(a, b, preferred_element_type=_F32)


def _whole(shape):
    zeros = (0,) * len(shape)
    return pl.BlockSpec(shape, lambda *_: zeros, pipeline_mode=pl.Buffered(1))


def _params(semantics):
    return pltpu.CompilerParams(dimension_semantics=semantics,
                                vmem_limit_bytes=VMEM_LIMIT)


def _pool_kernel(tiles_per_seq, x_ref, g_ref, w_ref, b_ref, sc_ref, o_ref, halo_ref):
    i = pl.program_id(0)
    t_in_seq = i % tiles_per_seq

    @pl.when(t_in_seq == 0)
    def _():
        halo_ref[...] = jnp.zeros_like(halo_ref)

    x = x_ref[...]
    tm, d = x.shape
    h = _rmsnorm(x, g_ref[...])
    prev = halo_ref[...]
    halo_ref[...] = h[tm - POOL_HALO:, :]
    hext = jnp.concatenate([prev, h], axis=0)
    pos = t_in_seq * tm + lax.broadcasted_iota(jnp.int32, (tm, 1), 0)
    gw = d // len(POOL_WINDOWS)
    for gi, w in enumerate(POOL_WINDOWS):
        cols = slice(gi * gw, (gi + 1) * gw)
        s = hext[:, cols]
        k = 1
        while k < w:
            s = s + pltpu.roll(s, k, axis=0)
            k *= 2
        inv_cnt = 1.0 / jnp.minimum(pos + 1, w).astype(_F32)
        pooled = s[POOL_HALO:, :] * inv_cnt
        y = _dot((pooled - h[:, cols]).astype(_BF16), w_ref[gi])
        o_ref[:, cols] = x[:, cols] + (y + b_ref[:, cols]) * sc_ref[:, cols]


def _pool_layer(x2, g, w, b, sc, seq):
    t, d = x2.shape
    tm = ROW_TILE
    gw = d // len(POOL_WINDOWS)
    row = pl.BlockSpec((tm, d), lambda i: (i, 0))
    vec = _whole((1, d))
    return pl.pallas_call(
        functools.partial(_pool_kernel, seq // tm),
        grid=(t // tm,),
        in_specs=[row, vec, _whole((len(POOL_WINDOWS), gw, gw)), vec, vec],
        out_specs=row,
        out_shape=jax.ShapeDtypeStruct((t, d), _F32),
        scratch_shapes=[pltpu.VMEM((POOL_HALO, d), _F32)],
        compiler_params=_params(("arbitrary",)),
        name="pool_layer",
    )(x2, g, w, b, sc)


def _shift_rows(a, prev, k):
    rolled = pltpu.roll(a, k, axis=0)
    prolled = pltpu.roll(prev, k, axis=0)
    rows = lax.broadcasted_iota(jnp.int32, prev.shape, 0)
    head = jnp.where(rows < k, prolled, rolled[:SUBLANES, :])
    return jnp.concatenate([head, rolled[SUBLANES:, :]], axis=0)


def _ffn_kernel(tiles_per_seq, x_ref, g_ref, wg_ref, wv_ref, cw_ref, cb_ref, wo_ref,
                o_ref, carry_ref, u_ref):
    i = pl.program_id(0)

    @pl.when(i % tiles_per_seq == 0)
    def _():
        carry_ref[...] = jnp.zeros_like(carry_ref)

    x = x_ref[...]
    tm = x.shape[0]
    hb = _rmsnorm(x, g_ref[...]).astype(_BF16)
    f = wg_ref.shape[1]
    for c in range(f // FFN_COLS):
        cols = slice(c * FFN_COLS, (c + 1) * FFN_COLS)
        a = _dot(hb, wg_ref[:, cols])
        v = _dot(hb, wv_ref[:, cols])
        prev = carry_ref[:, cols]
        carry_ref[:, cols] = a[tm - SUBLANES:, :]
        a1 = _shift_rows(a, prev, 1)
        a2 = _shift_rows(a, prev, 2)
        z = (a2 * cw_ref[0:1, cols] + a1 * cw_ref[1:2, cols] + a * cw_ref[2:3, cols]
             + cb_ref[:, cols])
        u_ref[:, cols] = (z * jax.nn.sigmoid(z) * v).astype(_BF16)
    o_ref[...] = x + _dot(u_ref[...], wo_ref[...])


def _ffn_layer(x2, g, wg, wv, cw, cb, wo, seq):
    t, d = x2.shape
    f = wg.shape[1]
    tm = ROW_TILE
    row = pl.BlockSpec((tm, d), lambda i: (i, 0))
    return pl.pallas_call(
        functools.partial(_ffn_kernel, seq // tm),
        grid=(t // tm,),
        in_specs=[row, _whole((1, d)), _whole((d, f)), _whole((d, f)),
                  _whole((CONV_W, f)), _whole((1, f)), _whole((f, d))],
        out_specs=row,
        out_shape=jax.ShapeDtypeStruct((t, d), _F32),
        scratch_shapes=[pltpu.VMEM((SUBLANES, f), _F32), pltpu.VMEM((tm, f), _BF16)],
        compiler_params=_params(("arbitrary",)),
        name="ffn_layer",
    )(x2, g, wg, wv, cw, cb, wo)


def _head_mean_square(y, ones_ref):
    sq = y * y
    hi = sq.astype(_BF16)
    lo = (sq - hi.astype(_F32)).astype(_BF16)
    parts = []
    for j in range(y.shape[1] // MXU_COLS):
        cols = slice(j * MXU_COLS, (j + 1) * MXU_COLS)
        parts.append(_dot(hi[:, cols], ones_ref[...]) + _dot(lo[:, cols], ones_ref[...]))
    return jnp.concatenate(parts, axis=1) * (1.0 / HEAD_DIM)


def _qkv_kernel(x_ref, g_ref, w_ref, qn_ref, kn_ref, ones_ref, q_ref, k_ref, v_ref):
    x = x_ref[...]
    d = x.shape[1]
    hb = _rmsnorm(x, g_ref[...]).astype(_BF16)
    q = _dot(hb, w_ref[:, 0:d])
    q = q * lax.rsqrt(_head_mean_square(q, ones_ref) + EPS) * qn_ref[...]
    q_ref[...] = (q * (HEAD_DIM ** -0.5)).astype(_BF16)
    k = _dot(hb, w_ref[:, d:2 * d])
    k = k * lax.rsqrt(_head_mean_square(k, ones_ref) + EPS) * kn_ref[...]
    k_ref[...] = k.astype(_BF16)
    v_ref[...] = _dot(hb, w_ref[:, 2 * d:3 * d]).astype(_BF16)


def _qkv_layer(x2, g, w, qn, kn, ones):
    t, d = x2.shape
    tm = ROW_TILE
    row = pl.BlockSpec((tm, d), lambda i: (i, 0))
    vec = _whole((1, d))
    out = jax.ShapeDtypeStruct((t, d), _BF16)
    return pl.pallas_call(
        _qkv_kernel,
        grid=(t // tm,),
        in_specs=[row, vec, _whole((d, 3 * d)), vec, vec, _whole((MXU_COLS, MXU_COLS))],
        out_specs=[row, row, row],
        out_shape=[out, out, out],
        compiler_params=_params(("arbitrary",)),
        name="qkv_layer",
    )(x2, g, w, qn, kn, ones)


def _attn_kernel(x_ref, q_ref, kp_ref, kc_ref, vp_ref, vc_ref, bm_ref, wo_ref,
                 o_ref, kwin_ref, vwin_ref, att_ref):
    first = pl.program_id(1) == 0
    tq = q_ref.shape[0]
    d = q_ref.shape[1]
    kwin_ref[0:tq, :] = kp_ref[...]
    kwin_ref[tq:2 * tq, :] = kc_ref[...]
    vwin_ref[0:tq, :] = vp_ref[...]
    vwin_ref[tq:2 * tq, :] = vc_ref[...]
    lane = lax.broadcasted_iota(jnp.int32, (ATTN_SUB, LANES), 1)
    low_half = lane < HEAD_DIM
    col = lax.broadcasted_iota(jnp.int32, (1, ATTN_WIN), 1)
    for s in range(tq // ATTN_SUB):
        r0 = s * ATTN_SUB
        pen = jnp.where(jnp.logical_and(first, col < tq - r0), NEG_BIG, 0.0)

        def pair(p, carry, r0=r0, pen=pen):
            c0 = pl.multiple_of(p * LANES, LANES)
            q2 = q_ref[r0:r0 + ATTN_SUB, pl.ds(c0, LANES)]
            k2 = kwin_ref[r0:r0 + ATTN_WIN, pl.ds(c0, LANES)]
            v2 = vwin_ref[r0:r0 + ATTN_WIN, pl.ds(c0, LANES)]
            outs = []
            for hh in range(2):
                keep = low_half if hh == 0 else jnp.logical_not(low_half)
                qm = jnp.where(keep, q2, jnp.zeros_like(q2))
                sc = lax.dot_general(qm, k2, (((1,), (1,)), ((), ())),
                                     preferred_element_type=_F32)
                sc = sc + bm_ref[2 * p + hh] + pen
                m = jnp.max(sc, axis=-1, keepdims=True)
                e = jnp.exp(sc - m)
                l = jnp.sum(e, axis=-1, keepdims=True)
                outs.append(_dot(e.astype(_BF16), v2) * (1.0 / l))
            att_ref[r0:r0 + ATTN_SUB, pl.ds(c0, LANES)] = (
                jnp.where(low_half, outs[0], outs[1]).astype(_BF16))
            return carry

        lax.fori_loop(0, d // LANES, pair, 0)
    o_ref[...] = x_ref[...] + _dot(att_ref[...], wo_ref[...])


def _attn_layer(x3, q3, k3, v3, bm, wo):
    b, s, d = x3.shape
    tq = ROW_TILE
    cur = pl.BlockSpec((None, tq, d), lambda bi, i: (bi, i, 0))
    prev = pl.BlockSpec((None, tq, d), lambda bi, i: (bi, jnp.maximum(i - 1, 0), 0))
    return pl.pallas_call(
        _attn_kernel,
        grid=(b, s // tq),
        in_specs=[cur, cur, prev, cur, prev, cur,
                  _whole((N_HEADS, ATTN_SUB, ATTN_WIN)), _whole((d, d))],
        out_specs=cur,
        out_shape=jax.ShapeDtypeStruct((b, s, d), _F32),
        scratch_shapes=[pltpu.VMEM((2 * tq, d), _BF16), pltpu.VMEM((2 * tq, d), _BF16),
                        pltpu.VMEM((tq, d), _BF16)],
        compiler_params=_params(("arbitrary", "arbitrary")),
        name="attn_layer",
    )(x3, q3, k3, k3, v3, v3, bm, wo)


def _bias_mask(rel_table):
    i = jnp.arange(ATTN_SUB)[:, None]
    j = jnp.arange(ATTN_WIN)[None, :]
    dist = i + LEFT_CHUNKS * CHUNK - j
    idx = jnp.clip(dist, -REL_CLIP, REL_CLIP) + REL_CLIP
    qc = i // CHUNK
    kc = j // CHUNK
    valid = jnp.logical_and(kc >= qc, kc <= qc + LEFT_CHUNKS)
    return jnp.where(valid[None], rel_table.astype(_F32)[:, idx], NEG_BIG)


def kernel(x, mix_norm, ffn_norm, pool_w, pool_b, pool_scale, attn_wqkv, attn_q_norm,
           attn_k_norm, attn_rel_bias, attn_wo, ffn_w_gate, ffn_w_val, ffn_conv_w,
           ffn_conv_b, ffn_w_out):
    b, s, d = x.shape
    depth = mix_norm.shape[0]
    assert s % ROW_TILE == 0 and d % MXU_COLS == 0 and d == N_HEADS * HEAD_DIM
    assert ffn_w_gate.shape[2] % FFN_COLS == 0
    idx = jnp.arange(MXU_COLS) // HEAD_DIM
    ones = (idx[:, None] == idx[None, :]).astype(_BF16)
    x2 = x.reshape(b * s, d)
    for i in range(depth):
        j = i // 2
        g = mix_norm[i][None, :]
        if i % 2 == 0:
            x2 = _pool_layer(x2, g, pool_w[j].astype(_BF16), pool_b[j][None, :],
                             pool_scale[j][None, :], s)
        else:
            qn = jnp.tile(attn_q_norm[j], N_HEADS)[None, :]
            kn = jnp.tile(attn_k_norm[j], N_HEADS)[None, :]
            q, k, v = _qkv_layer(x2, g, attn_wqkv[j].astype(_BF16), qn, kn, ones)
            shp = (b, s, d)
            x2 = _attn_layer(x2.reshape(shp), q.reshape(shp), k.reshape(shp),
                             v.reshape(shp), _bias_mask(attn_rel_bias[j]),
                             attn_wo[j].astype(_BF16)).reshape(b * s, d)
        x2 = _ffn_layer(x2, ffn_norm[i][None, :], ffn_w_gate[i].astype(_BF16),
                        ffn_w_val[i].astype(_BF16), ffn_conv_w[i], ffn_conv_b[i][None, :],
                        ffn_w_out[i].astype(_BF16), s)
    return x2.reshape(b, s, d)
```

```python
import functools

import jax
import jax.numpy as jnp
from jax import lax
from jax.experimental import pallas as pl
from jax.experimental.pallas import tpu as pltpu

EPS = 1e-6
CHUNK = 64
LEFT_CHUNKS = 8
N_HEADS = 16
HEAD_DIM = 64
REL_CLIP = 256
POOL_WINDOWS = (2, 4, 8, 16)
CONV_W = 3
NEG_BIG = -1e30

SUBLANES = 8
LANES = 128
MXU_COLS = 256

ROW_TILE = 512
POOL_HALO = 16
ATTN_SUB = 128
ATTN_WIN = ATTN_SUB + LEFT_CHUNKS * CHUNK
FFN_COLS = 256
VMEM_LIMIT = 56 * 1024 * 1024

_BF16 = jnp.bfloat16
_F32 = jnp.float32


def _rmsnorm(x, g):
    ms = jnp.mean(x * x, axis=-1, keepdims=True)
    return x * lax.rsqrt(ms + EPS) * g


def _dot(a, b):
    return jnp.dot(a, b, preferred_element_type=_F32)


def _whole(shape):
    zeros = (0,) * len(shape)
    return pl.BlockSpec(shape, lambda *_: zeros, pipeline_mode=pl.Buffered(1))


def _params(semantics):
    return pltpu.CompilerParams(dimension_semantics=semantics,
                                vmem_limit_bytes=VMEM_LIMIT)


def _pool_kernel(tiles_per_seq, x_ref, g_ref, w_ref, b_ref, sc_ref, o_ref, halo_ref):
    i = pl.program_id(0)
    t_in_seq = i % tiles_per_seq

    @pl.when(t_in_seq == 0)
    def _():
        halo_ref[...] = jnp.zeros_like(halo_ref)

    x = x_ref[...]
    tm, d = x.shape
    h = _rmsnorm(x, g_ref[...])
    prev = halo_ref[...]
    halo_ref[...] = h[tm - POOL_HALO:, :]
    hext = jnp.concatenate([prev, h], axis=0)
    pos = t_in_seq * tm + lax.broadcasted_iota(jnp.int32, (tm, 1), 0)
    gw = d // len(POOL_WINDOWS)
    for gi, w in enumerate(POOL_WINDOWS):
        cols = slice(gi * gw, (gi + 1) * gw)
        s = hext[:, cols]
        k = 1
        while k < w:
            s = s + pltpu.roll(s, k, axis=0)
            k *= 2
        inv_cnt = 1.0 / jnp.minimum(pos + 1, w).astype(_F32)
        pooled = s[POOL_HALO:, :] * inv_cnt
        y = _dot((pooled - h[:, cols]).astype(_BF16), w_ref[gi])
        o_ref[:, cols] = x[:, cols] + (y + b_ref[:, cols]) * sc_ref[:, cols]


def _pool_layer(x2, g, w, b, sc, seq):
    t, d = x2.shape
    tm = ROW_TILE
    gw = d // len(POOL_WINDOWS)
    row = pl.BlockSpec((tm, d), lambda i: (i, 0))
    vec = _whole((1, d))
    return pl.pallas_call(
        functools.partial(_pool_kernel, seq // tm),
        grid=(t // tm,),
        in_specs=[row, vec, _whole((len(POOL_WINDOWS), gw, gw)), vec, vec],
        out_specs=row,
        out_shape=jax.ShapeDtypeStruct((t, d), _F32),
        scratch_shapes=[pltpu.VMEM((POOL_HALO, d), _F32)],
        compiler_params=_params(("arbitrary",)),
        name="pool_layer",
    )(x2, g, w, b, sc)


def _shift_rows(a, prev, k):
    rolled = pltpu.roll(a, k, axis=0)
    prolled = pltpu.roll(prev, k, axis=0)
    rows = lax.broadcasted_iota(jnp.int32, prev.shape, 0)
    head = jnp.where(rows < k, prolled, rolled[:SUBLANES, :])
    return jnp.concatenate([head, rolled[SUBLANES:, :]], axis=0)


def _ffn_kernel(tiles_per_seq, x_ref, g_ref, wg_ref, wv_ref, cw_ref, cb_ref, wo_ref,
                o_ref, carry_ref, u_ref):
    i = pl.program_id(0)

    @pl.when(i % tiles_per_seq == 0)
    def _():
        carry_ref[...] = jnp.zeros_like(carry_ref)

    x = x_ref[...]
    tm = x.shape[0]
    hb = _rmsnorm(x, g_ref[...]).astype(_BF16)
    f = wg_ref.shape[1]
    for c in range(f // FFN_COLS):
        cols = slice(c * FFN_COLS, (c + 1) * FFN_COLS)
        a = _dot(hb, wg_ref[:, cols])
        v = _dot(hb, wv_ref[:, cols])
        prev = carry_ref[:, cols]
        carry_ref[:, cols] = a[tm - SUBLANES:, :]
        a1 = _shift_rows(a, prev, 1)
        a2 = _shift_rows(a, prev, 2)
        z = (a2 * cw_ref[0:1, cols] + a1 * cw_ref[1:2, cols] + a * cw_ref[2:3, cols]
             + cb_ref[:, cols])
        u_ref[:, cols] = (z * jax.nn.sigmoid(z) * v).astype(_BF16)
    o_ref[...] = x + _dot(u_ref[...], wo_ref[...])


def _ffn_layer(x2, g, wg, wv, cw, cb, wo, seq):
    t, d = x2.shape
    f = wg.shape[1]
    tm = ROW_TILE
    row = pl.BlockSpec((tm, d), lambda i: (i, 0))
    return pl.pallas_call(
        functools.partial(_ffn_kernel, seq // tm),
        grid=(t // tm,),
        in_specs=[row, _whole((1, d)), _whole((d, f)), _whole((d, f)),
                  _whole((CONV_W, f)), _whole((1, f)), _whole((f, d))],
        out_specs=row,
        out_shape=jax.ShapeDtypeStruct((t, d), _F32),
        scratch_shapes=[pltpu.VMEM((SUBLANES, f), _F32), pltpu.VMEM((tm, f), _BF16)],
        compiler_params=_params(("arbitrary",)),
        name="ffn_layer",
    )(x2, g, wg, wv, cw, cb, wo)


def _head_mean_square(y, ones_ref):
    sq = y * y
    hi = sq.astype(_BF16)
    lo = (sq - hi.astype(_F32)).astype(_BF16)
    parts = []
    for j in range(y.shape[1] // MXU_COLS):
        cols = slice(j * MXU_COLS, (j + 1) * MXU_COLS)
        parts.append(_dot(hi[:, cols], ones_ref[...]) + _dot(lo[:, cols], ones_ref[...]))
    return jnp.concatenate(parts, axis=1) * (1.0 / HEAD_DIM)


def _qkv_kernel(x_ref, g_ref, w_ref, qn_ref, kn_ref, ones_ref, q_ref, k_ref, v_ref):
    x = x_ref[...]
    d = x.shape[1]
    hb = _rmsnorm(x, g_ref[...]).astype(_BF16)
    q = _dot(hb, w_ref[:, 0:d])
    q = q * lax.rsqrt(_head_mean_square(q, ones_ref) + EPS) * qn_ref[...]
    q_ref[...] = (q * (HEAD_DIM ** -0.5)).astype(_BF16)
    k = _dot(hb, w_ref[:, d:2 * d])
    k = k * lax.rsqrt(_head_mean_square(k, ones_ref) + EPS) * kn_ref[...]
    k_ref[...] = k.astype(_BF16)
    v_ref[...] = _dot(hb, w_ref[:, 2 * d:3 * d]).astype(_BF16)


def _qkv_layer(x2, g, w, qn, kn, ones):
    t, d = x2.shape
    tm = ROW_TILE
    row = pl.BlockSpec((tm, d), lambda i: (i, 0))
    vec = _whole((1, d))
    out = jax.ShapeDtypeStruct((t, d), _BF16)
    return pl.pallas_call(
        _qkv_kernel,
        grid=(t // tm,),
        in_specs=[row, vec, _whole((d, 3 * d)), vec, vec, _whole((MXU_COLS, MXU_COLS))],
        out_specs=[row, row, row],
        out_shape=[out, out, out],
        compiler_params=_params(("arbitrary",)),
        name="qkv_layer",
    )(x2, g, w, qn, kn, ones)


def _attn_kernel(x_ref, q_ref, kp_ref, kc_ref, vp_ref, vc_ref, bm_ref, wo_ref,
                 o_ref, kwin_ref, vwin_ref, att_ref, sc_ref):
    first = pl.program_id(1) == 0
    tq = q_ref.shape[0]
    d = q_ref.shape[1]
    kwin_ref[0:tq, :] = kp_ref[...]
    kwin_ref[tq:2 * tq, :] = kc_ref[...]
    vwin_ref[0:tq, :] = vp_ref[...]
    vwin_ref[tq:2 * tq, :] = vc_ref[...]
    lane = lax.broadcasted_iota(jnp.int32, (ATTN_SUB, LANES), 1)
    low_half = lane < HEAD_DIM
    col = lax.broadcasted_iota(jnp.int32, (1, ATTN_WIN), 1)
    def sub_block(s, carry):
        r0 = pl.multiple_of(s * ATTN_SUB, ATTN_SUB)
        pen = jnp.where(jnp.logical_and(first, col < tq - r0), NEG_BIG, 0.0)
        n_pairs = d // LANES
        for p in range(n_pairs):
            cols = slice(p * LANES, (p + 1) * LANES)
            q2 = q_ref[pl.ds(r0, ATTN_SUB), cols]
            k2 = kwin_ref[pl.ds(r0, ATTN_WIN), cols]
            for hh in range(2):
                keep = low_half if hh == 0 else jnp.logical_not(low_half)
                qm = jnp.where(keep, q2, jnp.zeros_like(q2))
                sc = lax.dot_general(qm, k2, (((1,), (1,)), ((), ())),
                                     preferred_element_type=_F32)
                sc_ref[2 * p + hh] = sc + bm_ref[2 * p + hh] + pen
        for p in range(n_pairs):
            cols = slice(p * LANES, (p + 1) * LANES)
            v2 = vwin_ref[pl.ds(r0, ATTN_WIN), cols]
            outs = []
            for hh in range(2):
                sc = sc_ref[2 * p + hh]
                m = jnp.max(sc, axis=-1, keepdims=True)
                e = jnp.exp(sc - m)
                l = jnp.sum(e, axis=-1, keepdims=True)
                outs.append(_dot(e.astype(_BF16), v2) * (1.0 / l))
            att_ref[pl.ds(r0, ATTN_SUB), cols] = (
                jnp.where(low_half, outs[0], outs[1]).astype(_BF16))
        return carry

    lax.fori_loop(0, tq // ATTN_SUB, sub_block, 0)
    o_ref[...] = x_ref[...] + _dot(att_ref[...], wo_ref[...])


def _attn_layer(x3, q3, k3, v3, bm, wo):
    b, s, d = x3.shape
    tq = ROW_TILE
    cur = pl.BlockSpec((None, tq, d), lambda bi, i: (bi, i, 0))
    prev = pl.BlockSpec((None, tq, d), lambda bi, i: (bi, jnp.maximum(i - 1, 0), 0))
    return pl.pallas_call(
        _attn_kernel,
        grid=(b, s // tq),
        in_specs=[cur, cur, prev, cur, prev, cur,
                  _whole((N_HEADS, ATTN_SUB, ATTN_WIN)), _whole((d, d))],
        out_specs=cur,
        out_shape=jax.ShapeDtypeStruct((b, s, d), _F32),
        scratch_shapes=[pltpu.VMEM((2 * tq, d), _BF16), pltpu.VMEM((2 * tq, d), _BF16),
                        pltpu.VMEM((tq, d), _BF16),
                        pltpu.VMEM((N_HEADS, ATTN_SUB, ATTN_WIN), _F32)],
        compiler_params=_params(("arbitrary", "arbitrary")),
        name="attn_layer",
    )(x3, q3, k3, k3, v3, v3, bm, wo)


BIAS_ROW = 768


def _bias_kernel(r_ref, o_ref):
    x = jnp.broadcast_to(r_ref[0], (ATTN_SUB, BIAS_ROW))
    y = pltpu.roll(x, BIAS_ROW - ATTN_SUB + 1, axis=1, stride=1, stride_axis=0)
    qc = lax.broadcasted_iota(jnp.int32, (ATTN_SUB, ATTN_WIN), 0) // CHUNK
    kc = lax.broadcasted_iota(jnp.int32, (ATTN_SUB, ATTN_WIN), 1) // CHUNK
    valid = jnp.logical_and(kc >= qc, kc <= qc + LEFT_CHUNKS)
    o_ref[0] = jnp.where(valid, y[:, :ATTN_WIN], NEG_BIG)


def _bias_mask(rel_table):
    h, n = rel_table.shape
    lo = n - 1 - REL_CLIP - (ATTN_SUB - 1) - (BIAS_ROW - (ATTN_SUB + ATTN_WIN - 1))
    rev = rel_table[:, lo:][:, ::-1].astype(_F32)
    pad = jnp.broadcast_to(rel_table[:, n - 1:n].astype(_F32), (h, BIAS_ROW - (n - lo)))
    r = jnp.concatenate([pad, rev], axis=1)[:, None, :]
    return pl.pallas_call(
        _bias_kernel,
        grid=(h,),
        in_specs=[pl.BlockSpec((1, 1, BIAS_ROW), lambda i: (i, 0, 0))],
        out_specs=pl.BlockSpec((1, ATTN_SUB, ATTN_WIN), lambda i: (i, 0, 0)),
        out_shape=jax.ShapeDtypeStruct((h, ATTN_SUB, ATTN_WIN), _F32),
        compiler_params=_params(("arbitrary",)),
        name="bias_table",
    )(r)


def kernel(x, mix_norm, ffn_norm, pool_w, pool_b, pool_scale, attn_wqkv, attn_q_norm,
           attn_k_norm, attn_rel_bias, attn_wo, ffn_w_gate, ffn_w_val, ffn_conv_w,
           ffn_conv_b, ffn_w_out):
    b, s, d = x.shape
    depth = mix_norm.shape[0]
    assert s % ROW_TILE == 0 and d % MXU_COLS == 0 and d == N_HEADS * HEAD_DIM
    assert ffn_w_gate.shape[2] % FFN_COLS == 0
    idx = jnp.arange(MXU_COLS) // HEAD_DIM
    ones = (idx[:, None] == idx[None, :]).astype(_BF16)
    x2 = x.reshape(b * s, d)
    for i in range(depth):
        j = i // 2
        g = mix_norm[i][None, :]
        if i % 2 == 0:
            x2 = _pool_layer(x2, g, pool_w[j].astype(_BF16), pool_b[j][None, :],
                             pool_scale[j][None, :], s)
        else:
            qn = jnp.tile(attn_q_norm[j], N_HEADS)[None, :]
            kn = jnp.tile(attn_k_norm[j], N_HEADS)[None, :]
            q, k, v = _qkv_layer(x2, g, attn_wqkv[j].astype(_BF16), qn, kn, ones)
            shp = (b, s, d)
            x2 = _attn_layer(x2.reshape(shp), q.reshape(shp), k.reshape(shp),
                             v.reshape(shp), _bias_mask(attn_rel_bias[j]),
                             attn_wo[j].astype(_BF16)).reshape(b * s, d)
        x2 = _ffn_layer(x2, ffn_norm[i][None, :], ffn_w_gate[i].astype(_BF16),
                        ffn_w_val[i].astype(_BF16), ffn_conv_w[i], ffn_conv_b[i][None, :],
                        ffn_w_out[i].astype(_BF16), s)
    return x2.reshape(b, s, d)
```

```python
import functools

import jax
import jax.numpy as jnp
from jax import lax
from jax.experimental import pallas as pl
from jax.experimental.pallas import tpu as pltpu

EPS = 1e-6
CHUNK = 64
LEFT_CHUNKS = 8
N_HEADS = 16
HEAD_DIM = 64
REL_CLIP = 256
POOL_WINDOWS = (2, 4, 8, 16)
CONV_W = 3
NEG_BIG = -1e30
LOG2E = 1.4426950408889634

SUBLANES = 8
LANES = 128
MXU_COLS = 256

ROW_TILE = 512
POOL_HALO = 16
ATTN_SUB = 128
ATTN_WIN = ATTN_SUB + LEFT_CHUNKS * CHUNK
FFN_COLS = 256
VMEM_LIMIT = 56 * 1024 * 1024

_BF16 = jnp.bfloat16
_F32 = jnp.float32


def _rmsnorm(x, g):
    ms = jnp.mean(x * x, axis=-1, keepdims=True)
    return x * lax.rsqrt(ms + EPS) * g


def _dot(a, b):
    return jnp.dot(a, b, preferred_element_type=_F32)


def _whole(shape):
    zeros = (0,) * len(shape)
    return pl.BlockSpec(shape, lambda *_: zeros, pipeline_mode=pl.Buffered(1))


def _params(semantics):
    return pltpu.CompilerParams(dimension_semantics=semantics,
                                vmem_limit_bytes=VMEM_LIMIT)


def _pool_stages(t_in_seq, x_ref, g_ref, w_ref, b_ref, sc_ref, hext_ref, dst_ref):
    tm, d = x_ref.shape
    gw = d // len(POOL_WINDOWS)

    def norm():
        hext_ref[0:POOL_HALO, :] = hext_ref[tm:tm + POOL_HALO, :]
        hext_ref[POOL_HALO:, :] = _rmsnorm(x_ref[...], g_ref[...])

    def group(gi, w):
        cols = slice(gi * gw, (gi + 1) * gw)
        s = hext_ref[:, cols]
        k = 1
        while k < w:
            s = s + pltpu.roll(s, k, axis=0)
            k *= 2
        pos = t_in_seq * tm + lax.broadcasted_iota(jnp.int32, (tm, 1), 0)
        inv_cnt = 1.0 / jnp.minimum(pos + 1, w).astype(_F32)
        pooled = s[POOL_HALO:, :] * inv_cnt
        y = _dot((pooled - hext_ref[POOL_HALO:, cols]).astype(_BF16), w_ref[gi])
        dst_ref[:, cols] = x_ref[:, cols] + (y + b_ref[:, cols]) * sc_ref[:, cols]

    return [norm] + [functools.partial(group, gi, w) for gi, w in enumerate(POOL_WINDOWS)]


def _shift_rows(a, prev, k):
    rolled = pltpu.roll(a, k, axis=0)
    prolled = pltpu.roll(prev, k, axis=0)
    rows = lax.broadcasted_iota(jnp.int32, prev.shape, 0)
    head = jnp.where(rows < k, prolled, rolled[:SUBLANES, :])
    return jnp.concatenate([head, rolled[SUBLANES:, :]], axis=0)


def _ffn_body(x, g_ref, wg_ref, wv_ref, cw_ref, cb_ref, wo_ref, o_ref, carry_ref, u_ref,
              side=()):
    tm = x.shape[0]
    hb = _rmsnorm(x, g_ref[...]).astype(_BF16)
    f = wg_ref.shape[1]
    side = list(side)
    for c in range(f // FFN_COLS):
        cols = slice(c * FFN_COLS, (c + 1) * FFN_COLS)
        a = _dot(hb, wg_ref[:, cols])
        v = _dot(hb, wv_ref[:, cols])
        prev = carry_ref[:, cols]
        carry_ref[:, cols] = a[tm - SUBLANES:, :]
        a1 = _shift_rows(a, prev, 1)
        a2 = _shift_rows(a, prev, 2)
        z = (a2 * cw_ref[0:1, cols] + a1 * cw_ref[1:2, cols] + a * cw_ref[2:3, cols]
             + cb_ref[:, cols])
        u_ref[:, cols] = (z * jax.nn.sigmoid(z) * v).astype(_BF16)
    d = x.shape[1]
    for c in range(d // FFN_COLS):
        if side:
            side.pop(0)()
        cols = slice(c * FFN_COLS, (c + 1) * FFN_COLS)
        o_ref[:, cols] = x[:, cols] + _dot(u_ref[...], wo_ref[:, cols])
    for thunk in side:
        thunk()


def _ffn_kernel(tiles_per_seq, x_ref, g_ref, wg_ref, wv_ref, cw_ref, cb_ref, wo_ref,
                o_ref, carry_ref, u_ref):
    @pl.when(pl.program_id(0) % tiles_per_seq == 0)
    def _():
        carry_ref[...] = jnp.zeros_like(carry_ref)

    _ffn_body(x_ref[...], g_ref, wg_ref, wv_ref, cw_ref, cb_ref, wo_ref, o_ref,
              carry_ref, u_ref)


def _pool_ffn_kernel(tiles_per_seq, x0_ref, xn_ref, pg_ref, pw_ref, pb_ref, ps_ref,
                     g_ref, wg_ref, wv_ref, cw_ref, cb_ref, wo_ref,
                     o_ref, hext_ref, x1_ref, carry_ref, u_ref):
    i = pl.program_id(0)
    tm = o_ref.shape[0]
    stages = functools.partial(_pool_stages, g_ref=pg_ref, w_ref=pw_ref, b_ref=pb_ref,
                               sc_ref=ps_ref, hext_ref=hext_ref)

    @pl.when(i == 0)
    def _():
        hext_ref[tm:, :] = jnp.zeros((POOL_HALO, hext_ref.shape[1]), _F32)
        for thunk in stages(0, x0_ref, dst_ref=x1_ref.at[0]):
            thunk()

    nxt = i + 1

    @pl.when(nxt % tiles_per_seq == 0)
    def _():
        hext_ref[tm:, :] = jnp.zeros((POOL_HALO, hext_ref.shape[1]), _F32)

    @pl.when(i % tiles_per_seq == 0)
    def _():
        carry_ref[...] = jnp.zeros_like(carry_ref)

    def step(cur):
        side = stages(nxt % tiles_per_seq, xn_ref, dst_ref=x1_ref.at[1 - cur])
        _ffn_body(x1_ref[cur], g_ref, wg_ref, wv_ref, cw_ref, cb_ref, wo_ref, o_ref,
                  carry_ref, u_ref, side=side)

    pl.when(i % 2 == 0)(functools.partial(step, 0))
    pl.when(i % 2 == 1)(functools.partial(step, 1))


def _ffn_layer(x2, seq, g, wg, wv, cw, cb, wo, pool=None):
    t, d = x2.shape
    f = wg.shape[1]
    tm = ROW_TILE
    n_tiles = t // tm
    row = pl.BlockSpec((tm, d), lambda i: (i, 0))
    vec = _whole((1, d))
    specs = [vec, _whole((d, f)), _whole((d, f)), _whole((CONV_W, f)), _whole((1, f)),
             _whole((f, d))]
    args = [g, wg, wv, cw, cb, wo]
    scratch = [pltpu.VMEM((SUBLANES, f), _F32), pltpu.VMEM((tm, f), _BF16)]
    if pool is None:
        body, name = _ffn_kernel, "ffn_layer"
        specs = [row] + specs
        args = [x2] + args
    else:
        body, name = _pool_ffn_kernel, "pool_ffn_layer"
        gw = d // len(POOL_WINDOWS)
        first = pl.BlockSpec((tm, d), lambda i: (0, 0))
        ahead = pl.BlockSpec((tm, d), lambda i: (jnp.minimum(i + 1, n_tiles - 1), 0))
        specs = [first, ahead, vec, _whole((len(POOL_WINDOWS), gw, gw)), vec, vec] + specs
        args = [x2, x2] + list(pool) + args
        scratch = [pltpu.VMEM((POOL_HALO + tm, d), _F32), pltpu.VMEM((2, tm, d), _F32)] + scratch
    return pl.pallas_call(
        functools.partial(body, seq // tm),
        grid=(n_tiles,),
        in_specs=specs,
        out_specs=row,
        out_shape=jax.ShapeDtypeStruct((t, d), _F32),
        scratch_shapes=scratch,
        compiler_params=_params(("arbitrary",)),
        name=name,
    )(*args)


def _head_rms_scale(y):
    low = lax.broadcasted_iota(jnp.int32, (1, LANES), 1) < HEAD_DIM
    sq = y * y
    parts = []
    for j in range(y.shape[1] // LANES):
        blk = sq[:, j * LANES:(j + 1) * LANES]
        s_lo = jnp.sum(jnp.where(low, blk, 0.0), axis=-1, keepdims=True)
        s_hi = jnp.sum(jnp.where(low, 0.0, blk), axis=-1, keepdims=True)
        r_lo = lax.rsqrt(s_lo * (1.0 / HEAD_DIM) + EPS)
        r_hi = lax.rsqrt(s_hi * (1.0 / HEAD_DIM) + EPS)
        parts.append(jnp.where(low, r_lo, r_hi))
    return jnp.concatenate(parts, axis=1)


def _qkv_kernel(x_ref, g_ref, w_ref, wvt_ref, qn_ref, kn_ref, q_ref, k_ref, vt_ref):
    x = x_ref[...]
    d = x.shape[1]
    hb = _rmsnorm(x, g_ref[...]).astype(_BF16)
    q = _dot(hb, w_ref[:, 0:d])
    q = q * _head_rms_scale(q) * qn_ref[...]
    q_ref[...] = (q * (HEAD_DIM ** -0.5 * LOG2E)).astype(_BF16)
    k = _dot(hb, w_ref[:, d:2 * d])
    k = k * _head_rms_scale(k) * kn_ref[...]
    k_ref[...] = k.astype(_BF16)
    vt_ref[...] = lax.dot_general(wvt_ref[...], hb, (((1,), (1,)), ((), ())),
                                  preferred_element_type=_F32).astype(_BF16)


def _qkv_layer(x2, g, w, wvt, qn, kn, batch):
    t, d = x2.shape
    tm = ROW_TILE
    tiles_per_seq = t // batch // tm
    row = pl.BlockSpec((tm, d), lambda i: (i, 0))
    col = pl.BlockSpec((None, d, tm), lambda i: (i // tiles_per_seq, 0, i % tiles_per_seq))
    vec = _whole((1, d))
    out = jax.ShapeDtypeStruct((t, d), _BF16)
    return pl.pallas_call(
        _qkv_kernel,
        grid=(t // tm,),
        in_specs=[row, vec, _whole((d, 2 * d)), _whole((d, d)), vec, vec],
        out_specs=[row, row, col],
        out_shape=[out, out, jax.ShapeDtypeStruct((batch, d, t // batch), _BF16)],
        compiler_params=_params(("arbitrary",)),
        name="qkv_layer",
    )(x2, g, w, wvt, qn, kn)


def _attn_window(s, use_prev):
    n_cur = (s + 1) * ATTN_SUB
    n_prev = ATTN_WIN - n_cur
    return (0 if use_prev else n_prev), n_prev, n_cur


def _attn_scores(s, p, use_prev, q_ref, kp_ref, kc_ref, bt_ref, sc_ref):
    tq = q_ref.shape[0]
    j0, n_prev, n_cur = _attn_window(s, use_prev)
    cols = slice(p * LANES, (p + 1) * LANES)
    low_lanes = lax.broadcasted_iota(jnp.int32, (ATTN_SUB, LANES), 1) < HEAD_DIM
    q2 = q_ref[s * ATTN_SUB:(s + 1) * ATTN_SUB, cols]
    zero = jnp.zeros_like(q2)
    qq = jnp.concatenate([jnp.where(low_lanes, q2, zero),
                          jnp.where(low_lanes, zero, q2)], axis=0)
    trans_b = (((1,), (1,)), ((), ()))
    parts = []
    if use_prev:
        parts.append(lax.dot_general(kp_ref[tq - n_prev:tq, cols], qq, trans_b,
                                     preferred_element_type=_F32))
    parts.append(lax.dot_general(kc_ref[0:n_cur, cols], qq, trans_b,
                                 preferred_element_type=_F32))
    sc_ref[s % 2, p, j0:, :] = jnp.concatenate(parts, axis=0) + bt_ref[p, j0:, :]


def _attn_values(s, p, use_prev, vtp_ref, vtc_ref, sc_ref, att_ref):
    tq = att_ref.shape[0]
    j0, n_prev, n_cur = _attn_window(s, use_prev)
    cols = slice(p * LANES, (p + 1) * LANES)
    low_rows = lax.broadcasted_iota(jnp.int32, (LANES, ATTN_SUB), 0) < HEAD_DIM
    sc = sc_ref[s % 2, p, j0:, :]
    e = jnp.exp2(sc - jnp.max(sc, axis=0, keepdims=True)).astype(_BF16)
    ones = jnp.ones((2 * SUBLANES, n_cur), _BF16)
    ot = _dot(jnp.concatenate([vtc_ref[cols, 0:n_cur], ones], axis=0), e[n_prev - j0:, :])
    if use_prev:
        ones = jnp.ones((2 * SUBLANES, n_prev), _BF16)
        ot = ot + _dot(jnp.concatenate([vtp_ref[cols, tq - n_prev:tq], ones], axis=0),
                       e[0:n_prev, :])
    inv = 1.0 / ot[LANES:LANES + 1, :]
    out_t = jnp.where(low_rows, ot[0:LANES, 0:ATTN_SUB] * inv[:, 0:ATTN_SUB],
                      ot[0:LANES, ATTN_SUB:] * inv[:, ATTN_SUB:])
    att_ref[s * ATTN_SUB:(s + 1) * ATTN_SUB, cols] = out_t.T.astype(_BF16)


def _attn_kernel(x_ref, q_ref, kp_ref, kc_ref, vtp_ref, vtc_ref, bt_ref, wo_ref,
                 o_ref, att_ref, sc_ref):
    first = pl.program_id(1) == 0
    tq, d = q_ref.shape
    n_sub = tq // ATTN_SUB
    n_pairs = d // LANES

    def run(use_prev):
        scores = functools.partial(_attn_scores, use_prev=use_prev, q_ref=q_ref,
                                   kp_ref=kp_ref, kc_ref=kc_ref, bt_ref=bt_ref,
                                   sc_ref=sc_ref)
        values = functools.partial(_attn_values, use_prev=use_prev, vtp_ref=vtp_ref,
                                   vtc_ref=vtc_ref, sc_ref=sc_ref, att_ref=att_ref)
        for p in range(n_pairs):
            scores(0, p)
        for s in range(n_sub):
            for p in range(n_pairs):
                if s + 1 < n_sub:
                    scores(s + 1, p)
                values(s, p)

    pl.when(first)(functools.partial(run, False))
    pl.when(jnp.logical_not(first))(functools.partial(run, True))
    o_ref[...] = x_ref[...] + _dot(att_ref[...], wo_ref[...])


def _attn_layer(x3, q3, k3, vt3, bt, wo):
    b, s, d = x3.shape
    tq = ROW_TILE
    assert tq == LEFT_CHUNKS * CHUNK
    cur = pl.BlockSpec((None, tq, d), lambda bi, i: (bi, i, 0))
    prev = pl.BlockSpec((None, tq, d), lambda bi, i: (bi, jnp.maximum(i - 1, 0), 0))
    cur_t = pl.BlockSpec((None, d, tq), lambda bi, i: (bi, 0, i))
    prev_t = pl.BlockSpec((None, d, tq), lambda bi, i: (bi, 0, jnp.maximum(i - 1, 0)))
    n_pairs = d // LANES
    return pl.pallas_call(
        _attn_kernel,
        grid=(b, s // tq),
        in_specs=[cur, cur, prev, cur, prev_t, cur_t,
                  _whole((n_pairs, ATTN_WIN, 2 * ATTN_SUB)), _whole((d, d))],
        out_specs=cur,
        out_shape=jax.ShapeDtypeStruct((b, s, d), _F32),
        scratch_shapes=[pltpu.VMEM((tq, d), _BF16),
                        pltpu.VMEM((2, n_pairs, ATTN_WIN, 2 * ATTN_SUB), _F32)],
        compiler_params=_params(("arbitrary", "arbitrary")),
        name="attn_layer",
    )(x3, q3, k3, k3, vt3, vt3, bt, wo)


BIAS_ROW = 768


def _bias_kernel(c_ref, o_ref):
    kc = lax.broadcasted_iota(jnp.int32, (ATTN_WIN, ATTN_SUB), 0) // CHUNK
    qc = lax.broadcasted_iota(jnp.int32, (ATTN_WIN, ATTN_SUB), 1) // CHUNK
    valid = jnp.logical_and(kc >= qc, kc <= qc + LEFT_CHUNKS)
    halves = []
    for hh in range(2):
        x = jnp.broadcast_to(c_ref[hh], (ATTN_WIN, BIAS_ROW))
        y = pltpu.roll(x, BIAS_ROW - ATTN_WIN + 1, axis=1, stride=1, stride_axis=0)
        halves.append(jnp.where(valid, y[:, :ATTN_SUB] * LOG2E, NEG_BIG))
    o_ref[0] = jnp.concatenate(halves, axis=1)


def _bias_table(rel_table):
    h, n = rel_table.shape
    lo = n - 1 - REL_CLIP - (ATTN_SUB - 1)
    tail = jnp.broadcast_to(rel_table[:, n - 1:n], (h, BIAS_ROW - (n - lo)))
    c = jnp.concatenate([rel_table[:, lo:], tail], axis=1).astype(_F32)[:, None, :]
    return pl.pallas_call(
        _bias_kernel,
        grid=(h // 2,),
        in_specs=[pl.BlockSpec((2, 1, BIAS_ROW), lambda i: (i, 0, 0))],
        out_specs=pl.BlockSpec((1, ATTN_WIN, 2 * ATTN_SUB), lambda i: (i, 0, 0)),
        out_shape=jax.ShapeDtypeStruct((h // 2, ATTN_WIN, 2 * ATTN_SUB), _F32),
        compiler_params=_params(("arbitrary",)),
        name="bias_table",
    )(c)


def kernel(x, mix_norm, ffn_norm, pool_w, pool_b, pool_scale, attn_wqkv, attn_q_norm,
           attn_k_norm, attn_rel_bias, attn_wo, ffn_w_gate, ffn_w_val, ffn_conv_w,
           ffn_conv_b, ffn_w_out):
    b, s, d = x.shape
    depth = mix_norm.shape[0]
    assert s % ROW_TILE == 0 and d % MXU_COLS == 0 and d == N_HEADS * HEAD_DIM
    assert ffn_w_gate.shape[2] % FFN_COLS == 0
    x2 = x.reshape(b * s, d)
    for i in range(depth):
        j = i // 2
        g = mix_norm[i][None, :]
        pool = None
        if i % 2 == 0:
            pool = (g, pool_w[j].astype(_BF16), pool_b[j][None, :], pool_scale[j][None, :])
        else:
            qn = jnp.tile(attn_q_norm[j], N_HEADS)[None, :]
            kn = jnp.tile(attn_k_norm[j], N_HEADS)[None, :]
            wqk = attn_wqkv[j][:, :2 * d].astype(_BF16)
            wvt = attn_wqkv[j][:, 2 * d:].T.astype(_BF16)
            q, k, vt = _qkv_layer(x2, g, wqk, wvt, qn, kn, b)
            shp = (b, s, d)
            x2 = _attn_layer(x2.reshape(shp), q.reshape(shp), k.reshape(shp), vt,
                             _bias_table(attn_rel_bias[j]),
                             attn_wo[j].astype(_BF16)).reshape(b * s, d)
        x2 = _ffn_layer(x2, s, ffn_norm[i][None, :], ffn_w_gate[i].astype(_BF16),
                        ffn_w_val[i].astype(_BF16), ffn_conv_w[i], ffn_conv_b[i][None, :],
                        ffn_w_out[i].astype(_BF16), pool=pool)
    return x2.reshape(b, s, d)
```

```python
import functools

import jax
import jax.numpy as jnp
from jax import lax
from jax.experimental import pallas as pl
from jax.experimental.pallas import tpu as pltpu

EPS = 1e-6
CHUNK = 64
LEFT_CHUNKS = 8
N_HEADS = 16
HEAD_DIM = 64
REL_CLIP = 256
POOL_WINDOWS = (2, 4, 8, 16)
CONV_W = 3
NEG_BIG = -1e30
LOG2E = 1.4426950408889634

SUBLANES = 8
LANES = 128
MXU_COLS = 256

ROW_TILE = 512
FFN_ROW_TILE = 1024
POOL_HALO = 16
POOL_BLOCK = 128
ATTN_SUB = 128
ATTN_WIN = ATTN_SUB + LEFT_CHUNKS * CHUNK
FFN_COLS = 256
VMEM_LIMIT = 56 * 1024 * 1024

_BF16 = jnp.bfloat16
_F32 = jnp.float32


def _rmsnorm(x, g):
    ms = jnp.mean(x * x, axis=-1, keepdims=True)
    return x * lax.rsqrt(ms + EPS) * g


def _dot(a, b):
    return jnp.dot(a, b, preferred_element_type=_F32)


def _whole(shape):
    zeros = (0,) * len(shape)
    return pl.BlockSpec(shape, lambda *_: zeros, pipeline_mode=pl.Buffered(1))


def _params(semantics):
    return pltpu.CompilerParams(dimension_semantics=semantics,
                                vmem_limit_bytes=VMEM_LIMIT)


def _pool_kernel(tiles_per_seq, x_ref, g_ref, band_ref, w_ref, b_ref, sc_ref, o_ref,
                 hext_ref, hi_ref, lo_ref):
    tm, d = x_ref.shape
    t_in_seq = pl.program_id(0) % tiles_per_seq

    @pl.when(t_in_seq == 0)
    def _():
        hext_ref[tm:, :] = jnp.zeros((POOL_HALO, d), _F32)

    hext_ref[0:POOL_HALO, :] = hext_ref[tm:tm + POOL_HALO, :]
    hext_ref[POOL_HALO:, :] = _rmsnorm(x_ref[...], g_ref[...])
    hext = hext_ref[...]
    hi = hext.astype(_BF16)
    hi_ref[...] = hi
    lo_ref[...] = (hext - hi.astype(_F32)).astype(_BF16)

    gw = d // len(POOL_WINDOWS)
    blk = POOL_BLOCK
    for rb in range(tm // blk):
        rows = slice(rb * blk, (rb + 1) * blk)
        src = slice(rb * blk, (rb + 1) * blk + POOL_HALO)
        for gi in range(len(POOL_WINDOWS)):
            cols = slice(gi * gw, (gi + 1) * gw)
            band = band_ref[gi]
            o_ref[rows, cols] = _dot(band, hi_ref[src, cols]) + _dot(band, lo_ref[src, cols])
    for rb in range(tm // blk):
        rows = slice(rb * blk, (rb + 1) * blk)
        pos = t_in_seq * tm + rb * blk + lax.broadcasted_iota(jnp.int32, (blk, 1), 0)
        for gi, w in enumerate(POOL_WINDOWS):
            cols = slice(gi * gw, (gi + 1) * gw)
            inv_cnt = 1.0 / jnp.minimum(pos + 1, w).astype(_F32)
            h = hext_ref[rb * blk + POOL_HALO:(rb + 1) * blk + POOL_HALO, cols]
            y = _dot((o_ref[rows, cols] * inv_cnt - h).astype(_BF16), w_ref[gi])
            o_ref[rows, cols] = x_ref[rows, cols] + (y + b_ref[:, cols]) * sc_ref[:, cols]


def _pool_bands():
    i = jnp.arange(POOL_BLOCK)[:, None] + POOL_HALO
    k = jnp.arange(POOL_BLOCK + POOL_HALO)[None, :]
    return jnp.stack([jnp.logical_and(k <= i, k > i - w) for w in POOL_WINDOWS]).astype(_BF16)


def _pool_layer(x2, g, w, b, sc, seq):
    t, d = x2.shape
    tm = ROW_TILE
    gw = d // len(POOL_WINDOWS)
    n_g = len(POOL_WINDOWS)
    row = pl.BlockSpec((tm, d), lambda i: (i, 0))
    vec = _whole((1, d))
    return pl.pallas_call(
        functools.partial(_pool_kernel, seq // tm),
        grid=(t // tm,),
        in_specs=[row, vec, _whole((n_g, POOL_BLOCK, POOL_BLOCK + POOL_HALO)),
                  _whole((n_g, gw, gw)), vec, vec],
        out_specs=row,
        out_shape=jax.ShapeDtypeStruct((t, d), _F32),
        scratch_shapes=[pltpu.VMEM((POOL_HALO + tm, d), _F32),
                        pltpu.VMEM((POOL_HALO + tm, d), _BF16),
                        pltpu.VMEM((POOL_HALO + tm, d), _BF16)],
        compiler_params=_params(("arbitrary",)),
        name="pool_layer",
    )(x2, g, _pool_bands(), w, b, sc)


def _shift_rows(a, prev, k):
    rolled = pltpu.roll(a, k, axis=0)
    prolled = pltpu.roll(prev, k, axis=0)
    rows = lax.broadcasted_iota(jnp.int32, prev.shape, 0)
    head = jnp.where(rows < k, prolled, rolled[:SUBLANES, :])
    return jnp.concatenate([head, rolled[SUBLANES:, :]], axis=0)


def _ffn_kernel(tiles_per_seq, n_cast, *refs):
    x_ref, g_ref, wg_ref, wv_ref, cw_ref, cb_ref, wo_ref = refs[:7]
    cast_in = refs[7:7 + n_cast]
    o_ref = refs[7 + n_cast]
    cast_out = refs[8 + n_cast:8 + 2 * n_cast]
    carry_ref, u_ref = refs[8 + 2 * n_cast:]

    @pl.when(pl.program_id(0) % tiles_per_seq == 0)
    def _():
        carry_ref[...] = jnp.zeros_like(carry_ref)

    x = x_ref[...]
    tm = x.shape[0]
    hb = _rmsnorm(x, g_ref[...]).astype(_BF16)
    f = wg_ref.shape[1]
    for c in range(f // FFN_COLS):
        cols = slice(c * FFN_COLS, (c + 1) * FFN_COLS)
        a = _dot(hb, wg_ref[:, cols])
        v = _dot(hb, wv_ref[:, cols])
        prev = carry_ref[:, cols]
        carry_ref[:, cols] = a[tm - SUBLANES:, :]
        a1 = _shift_rows(a, prev, 1)
        a2 = _shift_rows(a, prev, 2)
        z = (a2 * cw_ref[0:1, cols] + a1 * cw_ref[1:2, cols] + a * cw_ref[2:3, cols]
             + cb_ref[:, cols])
        u_ref[:, cols] = (z * jax.nn.sigmoid(z) * v).astype(_BF16)
    o_ref[...] = x + _dot(u_ref[...], wo_ref[...])
    for src, dst in zip(cast_in, cast_out):
        dst[...] = src[...].astype(_BF16)


def _ffn_layer(x2, seq, g, wg, wv, cw, cb, wo, cast=()):
    t, d = x2.shape
    f = wg.shape[1]
    tm = FFN_ROW_TILE
    n_steps = t // tm
    row = pl.BlockSpec((tm, d), lambda i: (i, 0))
    slabs = [pl.BlockSpec((w.shape[0] // n_steps, w.shape[1]), lambda i: (i, 0))
             for w in cast]
    for w in cast:
        assert w.shape[0] % (n_steps * 2 * SUBLANES) == 0
    outs = pl.pallas_call(
        functools.partial(_ffn_kernel, seq // tm, len(cast)),
        grid=(n_steps,),
        in_specs=[row, _whole((1, d)), _whole((d, f)), _whole((d, f)),
                  _whole((CONV_W, f)), _whole((1, f)), _whole((f, d))] + slabs,
        out_specs=[row] + slabs,
        out_shape=[jax.ShapeDtypeStruct((t, d), _F32)]
        + [jax.ShapeDtypeStruct(w.shape, _BF16) for w in cast],
        scratch_shapes=[pltpu.VMEM((SUBLANES, f), _F32), pltpu.VMEM((tm, f), _BF16)],
        compiler_params=_params(("arbitrary",)),
        name="ffn_layer",
    )(x2, g, wg, wv, cw, cb, wo, *cast)
    return outs[0], list(outs[1:])


def _head_rms_scale(y):
    low = lax.broadcasted_iota(jnp.int32, (1, LANES), 1) < HEAD_DIM
    sq = y * y
    parts = []
    for j in range(y.shape[1] // LANES):
        blk = sq[:, j * LANES:(j + 1) * LANES]
        s_lo = jnp.sum(jnp.where(low, blk, 0.0), axis=-1, keepdims=True)
        s_hi = jnp.sum(jnp.where(low, 0.0, blk), axis=-1, keepdims=True)
        r_lo = lax.rsqrt(s_lo * (1.0 / HEAD_DIM) + EPS)
        r_hi = lax.rsqrt(s_hi * (1.0 / HEAD_DIM) + EPS)
        parts.append(jnp.where(low, r_lo, r_hi))
    return jnp.concatenate(parts, axis=1)


def _qkv_kernel(x_ref, g_ref, w_ref, wvt_ref, qn_ref, kn_ref, q_ref, k_ref, vt_ref):
    x = x_ref[...]
    d = x.shape[1]
    hb = _rmsnorm(x, g_ref[...]).astype(_BF16)
    q = _dot(hb, w_ref[:, 0:d])
    q = q * _head_rms_scale(q) * qn_ref[...]
    q_ref[...] = (q * (HEAD_DIM ** -0.5 * LOG2E)).astype(_BF16)
    k = _dot(hb, w_ref[:, d:2 * d])
    k = k * _head_rms_scale(k) * kn_ref[...]
    k_ref[...] = k.astype(_BF16)
    vt_ref[...] = lax.dot_general(wvt_ref[...], hb, (((1,), (1,)), ((), ())),
                                  preferred_element_type=_F32).astype(_BF16)


def _qkv_layer(x2, g, w, wvt, qn, kn, batch):
    t, d = x2.shape
    tm = ROW_TILE
    tiles_per_seq = t // batch // tm
    row = pl.BlockSpec((tm, d), lambda i: (i, 0))
    col = pl.BlockSpec((None, d, tm), lambda i: (i // tiles_per_seq, 0, i % tiles_per_seq))
    vec = _whole((1, d))
    out = jax.ShapeDtypeStruct((t, d), _BF16)
    return pl.pallas_call(
        _qkv_kernel,
        grid=(t // tm,),
        in_specs=[row, vec, _whole((d, 2 * d)), _whole((d, d)), vec, vec],
        out_specs=[row, row, col],
        out_shape=[out, out, jax.ShapeDtypeStruct((batch, d, t // batch), _BF16)],
        compiler_params=_params(("arbitrary",)),
        name="qkv_layer",
    )(x2, g, w, wvt, qn, kn)


def _attn_window(s, use_prev):
    n_cur = (s + 1) * ATTN_SUB
    n_prev = ATTN_WIN - n_cur
    return (0 if use_prev else n_prev), n_prev, n_cur


def _attn_scores(s, p, use_prev, q_ref, kp_ref, kc_ref, bt_ref, sc_ref):
    tq = q_ref.shape[0]
    j0, n_prev, n_cur = _attn_window(s, use_prev)
    cols = slice(p * LANES, (p + 1) * LANES)
    low_lanes = lax.broadcasted_iota(jnp.int32, (ATTN_SUB, LANES), 1) < HEAD_DIM
    q2 = q_ref[s * ATTN_SUB:(s + 1) * ATTN_SUB, cols]
    zero = jnp.zeros_like(q2)
    qq = jnp.concatenate([jnp.where(low_lanes, q2, zero),
                          jnp.where(low_lanes, zero, q2)], axis=0)
    trans_b = (((1,), (1,)), ((), ()))
    parts = []
    if use_prev:
        parts.append(lax.dot_general(kp_ref[tq - n_prev:tq, cols], qq, trans_b,
                                     preferred_element_type=_F32))
    parts.append(lax.dot_general(kc_ref[0:n_cur, cols], qq, trans_b,
                                 preferred_element_type=_F32))
    sc_ref[s % 2, p, j0:, :] = jnp.concatenate(parts, axis=0) + bt_ref[p, j0:, :]


def _attn_values(s, p, use_prev, vtp_ref, vtc_ref, sc_ref, att_ref):
    tq = att_ref.shape[0]
    j0, n_prev, n_cur = _attn_window(s, use_prev)
    cols = slice(p * LANES, (p + 1) * LANES)
    low_rows = lax.broadcasted_iota(jnp.int32, (LANES, ATTN_SUB), 0) < HEAD_DIM
    sc = sc_ref[s % 2, p, j0:, :]
    e = jnp.exp2(sc - jnp.max(sc, axis=0, keepdims=True)).astype(_BF16)
    ones = jnp.ones((2 * SUBLANES, n_cur), _BF16)
    ot = _dot(jnp.concatenate([vtc_ref[cols, 0:n_cur], ones], axis=0), e[n_prev - j0:, :])
    if use_prev:
        ones = jnp.ones((2 * SUBLANES, n_prev), _BF16)
        ot = ot + _dot(jnp.concatenate([vtp_ref[cols, tq - n_prev:tq], ones], axis=0),
                       e[0:n_prev, :])
    inv = 1.0 / ot[LANES:LANES + 1, :]
    out_t = jnp.where(low_rows, ot[0:LANES, 0:ATTN_SUB] * inv[:, 0:ATTN_SUB],
                      ot[0:LANES, ATTN_SUB:] * inv[:, ATTN_SUB:])
    att_ref[s * ATTN_SUB:(s + 1) * ATTN_SUB, cols] = out_t.T.astype(_BF16)


def _attn_kernel(x_ref, q_ref, kp_ref, kc_ref, vtp_ref, vtc_ref, bt_ref, wo_ref,
                 o_ref, att_ref, sc_ref):
    first = pl.program_id(1) == 0
    tq, d = q_ref.shape
    n_sub = tq // ATTN_SUB
    n_pairs = d // LANES

    def run(use_prev):
        scores = functools.partial(_attn_scores, use_prev=use_prev, q_ref=q_ref,
                                   kp_ref=kp_ref, kc_ref=kc_ref, bt_ref=bt_ref,
                                   sc_ref=sc_ref)
        values = functools.partial(_attn_values, use_prev=use_prev, vtp_ref=vtp_ref,
                                   vtc_ref=vtc_ref, sc_ref=sc_ref, att_ref=att_ref)
        for p in range(n_pairs):
            scores(0, p)
        for s in range(n_sub):
            for p in range(n_pairs):
                if s + 1 < n_sub:
                    scores(s + 1, p)
                values(s, p)

    pl.when(first)(functools.partial(run, False))
    pl.when(jnp.logical_not(first))(functools.partial(run, True))
    o_ref[...] = x_ref[...] + _dot(att_ref[...], wo_ref[...])


def _attn_layer(x3, q3, k3, vt3, bt, wo):
    b, s, d = x3.shape
    tq = ROW_TILE
    assert tq == LEFT_CHUNKS * CHUNK
    cur = pl.BlockSpec((None, tq, d), lambda bi, i: (bi, i, 0))
    prev = pl.BlockSpec((None, tq, d), lambda bi, i: (bi, jnp.maximum(i - 1, 0), 0))
    cur_t = pl.BlockSpec((None, d, tq), lambda bi, i: (bi, 0, i))
    prev_t = pl.BlockSpec((None, d, tq), lambda bi, i: (bi, 0, jnp.maximum(i - 1, 0)))
    n_pairs = d // LANES
    return pl.pallas_call(
        _attn_kernel,
        grid=(b, s // tq),
        in_specs=[cur, cur, prev, cur, prev_t, cur_t,
                  _whole((n_pairs, ATTN_WIN, 2 * ATTN_SUB)), _whole((d, d))],
        out_specs=cur,
        out_shape=jax.ShapeDtypeStruct((b, s, d), _F32),
        scratch_shapes=[pltpu.VMEM((tq, d), _BF16),
                        pltpu.VMEM((2, n_pairs, ATTN_WIN, 2 * ATTN_SUB), _F32)],
        compiler_params=_params(("arbitrary", "arbitrary")),
        name="attn_layer",
    )(x3, q3, k3, k3, vt3, vt3, bt, wo)


BIAS_ROW = 768


def _bias_kernel(c_ref, o_ref):
    kc = lax.broadcasted_iota(jnp.int32, (ATTN_WIN, ATTN_SUB), 0) // CHUNK
    qc = lax.broadcasted_iota(jnp.int32, (ATTN_WIN, ATTN_SUB), 1) // CHUNK
    valid = jnp.logical_and(kc >= qc, kc <= qc + LEFT_CHUNKS)
    halves = []
    for hh in range(2):
        x = jnp.broadcast_to(c_ref[hh], (ATTN_WIN, BIAS_ROW))
        y = pltpu.roll(x, BIAS_ROW - ATTN_WIN + 1, axis=1, stride=1, stride_axis=0)
        halves.append(jnp.where(valid, y[:, :ATTN_SUB] * LOG2E, NEG_BIG))
    o_ref[0] = jnp.concatenate(halves, axis=1)


def _bias_table(rel_table):
    h, n = rel_table.shape
    lo = n - 1 - REL_CLIP - (ATTN_SUB - 1)
    tail = jnp.broadcast_to(rel_table[:, n - 1:n], (h, BIAS_ROW - (n - lo)))
    c = jnp.concatenate([rel_table[:, lo:], tail], axis=1).astype(_F32)[:, None, :]
    return pl.pallas_call(
        _bias_kernel,
        grid=(h // 2,),
        in_specs=[pl.BlockSpec((2, 1, BIAS_ROW), lambda i: (i, 0, 0))],
        out_specs=pl.BlockSpec((1, ATTN_WIN, 2 * ATTN_SUB), lambda i: (i, 0, 0)),
        out_shape=jax.ShapeDtypeStruct((h // 2, ATTN_WIN, 2 * ATTN_SUB), _F32),
        compiler_params=_params(("arbitrary",)),
        name="bias_table",
    )(c)


def kernel(x, mix_norm, ffn_norm, pool_w, pool_b, pool_scale, attn_wqkv, attn_q_norm,
           attn_k_norm, attn_rel_bias, attn_wo, ffn_w_gate, ffn_w_val, ffn_conv_w,
           ffn_conv_b, ffn_w_out):
    b, s, d = x.shape
    depth = mix_norm.shape[0]
    assert s % ROW_TILE == 0 and d % MXU_COLS == 0 and d == N_HEADS * HEAD_DIM
    assert ffn_w_gate.shape[2] % FFN_COLS == 0
    f = ffn_w_gate.shape[2]
    x2 = x.reshape(b * s, d)

    def ffn_weights(i):
        return [ffn_w_gate[i], ffn_w_val[i], ffn_w_out[i].reshape(d, f)]

    ffn_b = [w.astype(_BF16) for w in ffn_weights(0)]
    attn_b = None
    for i in range(depth):
        j = i // 2
        g = mix_norm[i][None, :]
        if i % 2 == 0:
            x2 = _pool_layer(x2, g, pool_w[j].astype(_BF16), pool_b[j][None, :],
                             pool_scale[j][None, :], s)
        else:
            wqkv_b, wo_b = attn_b
            qn = jnp.tile(attn_q_norm[j], N_HEADS)[None, :]
            kn = jnp.tile(attn_k_norm[j], N_HEADS)[None, :]
            q, k, vt = _qkv_layer(x2, g, wqkv_b, wqkv_b[:, 2 * d:].T, qn, kn, b)
            shp = (b, s, d)
            x2 = _attn_layer(x2.reshape(shp), q.reshape(shp), k.reshape(shp), vt,
                             _bias_table(attn_rel_bias[j]), wo_b).reshape(b * s, d)
        pending = []
        if i + 1 < depth:
            pending += ffn_weights(i + 1)
            if (i + 1) % 2 == 1:
                pending += [attn_wqkv[(i + 1) // 2], attn_wo[(i + 1) // 2]]
        x2, done = _ffn_layer(x2, s, ffn_norm[i][None, :], ffn_b[0], ffn_b[1], ffn_conv_w[i],
                              ffn_conv_b[i][None, :], ffn_b[2].reshape(f, d), cast=pending)
        ffn_b, attn_b = done[:3], done[3:]
    return x2.reshape(b, s, d)
```

```python
import functools

import jax
import jax.numpy as jnp
from jax import lax
from jax.experimental import pallas as pl
from jax.experimental.pallas import tpu as pltpu

EPS = 1e-6
CHUNK = 64
LEFT_CHUNKS = 8
N_HEADS = 16
HEAD_DIM = 64
REL_CLIP = 256
POOL_WINDOWS = (2, 4, 8, 16)
CONV_W = 3
NEG_BIG = -1e30
LOG2E = 1.4426950408889634

SUBLANES = 8
LANES = 128
MXU_COLS = 256

ROW_TILE = 512
FFN_ROW_TILE = 1024
POOL_ROW_TILE = 1024
POOL_HALO = 16
POOL_BLOCK = 128
ATTN_SUB = 128
ATTN_WIN = ATTN_SUB + LEFT_CHUNKS * CHUNK
FFN_COLS = 256
VMEM_LIMIT = 56 * 1024 * 1024

_BF16 = jnp.bfloat16
_F32 = jnp.float32


def _rmsnorm(x, g):
    ms = jnp.mean(x * x, axis=-1, keepdims=True)
    return x * lax.rsqrt(ms + EPS) * g


def _dot(a, b):
    return jnp.dot(a, b, preferred_element_type=_F32)


def _whole(shape):
    zeros = (0,) * len(shape)
    return pl.BlockSpec(shape, lambda *_: zeros, pipeline_mode=pl.Buffered(1))


def _params(semantics):
    return pltpu.CompilerParams(dimension_semantics=semantics,
                                vmem_limit_bytes=VMEM_LIMIT)


def _pool_kernel(tiles_per_seq, x_ref, g_ref, band_ref, w_ref, b_ref, sc_ref, o_ref,
                 hext_ref, hi_ref, lo_ref):
    tm, d = x_ref.shape
    t_in_seq = pl.program_id(0) % tiles_per_seq

    @pl.when(t_in_seq == 0)
    def _():
        hext_ref[tm:, :] = jnp.zeros((POOL_HALO, d), _F32)

    hext_ref[0:POOL_HALO, :] = hext_ref[tm:tm + POOL_HALO, :]
    hext_ref[POOL_HALO:, :] = _rmsnorm(x_ref[...], g_ref[...])
    hext = hext_ref[...]
    hi = hext.astype(_BF16)
    hi_ref[...] = hi
    lo_ref[...] = (hext - hi.astype(_F32)).astype(_BF16)

    gw = d // len(POOL_WINDOWS)
    blk = POOL_BLOCK
    for rb in range(tm // blk):
        rows = slice(rb * blk, (rb + 1) * blk)
        src = slice(rb * blk, (rb + 1) * blk + POOL_HALO)
        for gi in range(len(POOL_WINDOWS)):
            cols = slice(gi * gw, (gi + 1) * gw)
            band = band_ref[gi]
            o_ref[rows, cols] = _dot(band, hi_ref[src, cols]) + _dot(band, lo_ref[src, cols])
    for rb in range(tm // blk):
        rows = slice(rb * blk, (rb + 1) * blk)
        pos = t_in_seq * tm + rb * blk + lax.broadcasted_iota(jnp.int32, (blk, 1), 0)
        for gi, w in enumerate(POOL_WINDOWS):
            cols = slice(gi * gw, (gi + 1) * gw)
            inv_cnt = 1.0 / jnp.minimum(pos + 1, w).astype(_F32)
            h = hext_ref[rb * blk + POOL_HALO:(rb + 1) * blk + POOL_HALO, cols]
            y = _dot((o_ref[rows, cols] * inv_cnt - h).astype(_BF16), w_ref[gi])
            o_ref[rows, cols] = x_ref[rows, cols] + (y + b_ref[:, cols]) * sc_ref[:, cols]


def _pool_bands():
    i = jnp.arange(POOL_BLOCK)[:, None] + POOL_HALO
    k = jnp.arange(POOL_BLOCK + POOL_HALO)[None, :]
    return jnp.stack([jnp.logical_and(k <= i, k > i - w) for w in POOL_WINDOWS]).astype(_BF16)


def _pool_layer(x2, g, w, b, sc, seq):
    t, d = x2.shape
    tm = POOL_ROW_TILE
    gw = d // len(POOL_WINDOWS)
    n_g = len(POOL_WINDOWS)
    row = pl.BlockSpec((tm, d), lambda i: (i, 0))
    vec = _whole((1, d))
    return pl.pallas_call(
        functools.partial(_pool_kernel, seq // tm),
        grid=(t // tm,),
        in_specs=[row, vec, _whole((n_g, POOL_BLOCK, POOL_BLOCK + POOL_HALO)),
                  _whole((n_g, gw, gw)), vec, vec],
        out_specs=row,
        out_shape=jax.ShapeDtypeStruct((t, d), _F32),
        scratch_shapes=[pltpu.VMEM((POOL_HALO + tm, d), _F32),
                        pltpu.VMEM((POOL_HALO + tm, d), _BF16),
                        pltpu.VMEM((POOL_HALO + tm, d), _BF16)],
        compiler_params=_params(("arbitrary",)),
        name="pool_layer",
    )(x2, g, _pool_bands(), w, b, sc)


def _shift_rows(a, prev, k):
    rolled = pltpu.roll(a, k, axis=0)
    prolled = pltpu.roll(prev, k, axis=0)
    rows = lax.broadcasted_iota(jnp.int32, prev.shape, 0)
    head = jnp.where(rows < k, prolled, rolled[:SUBLANES, :])
    return jnp.concatenate([head, rolled[SUBLANES:, :]], axis=0)


def _ffn_kernel(tiles_per_seq, n_cast, *refs):
    x_ref, g_ref, wg_ref, wv_ref, cw_ref, cb_ref, wo_ref = refs[:7]
    cast_in = refs[7:7 + n_cast]
    o_ref = refs[7 + n_cast]
    cast_out = refs[8 + n_cast:8 + 2 * n_cast]
    carry_ref, u_ref = refs[8 + 2 * n_cast:]

    @pl.when(pl.program_id(0) % tiles_per_seq == 0)
    def _():
        carry_ref[...] = jnp.zeros_like(carry_ref)

    x = x_ref[...]
    tm = x.shape[0]
    hb = _rmsnorm(x, g_ref[...]).astype(_BF16)
    f = wg_ref.shape[1]
    for c in range(f // FFN_COLS):
        cols = slice(c * FFN_COLS, (c + 1) * FFN_COLS)
        a = _dot(hb, wg_ref[:, cols])
        v = _dot(hb, wv_ref[:, cols])
        prev = carry_ref[:, cols]
        carry_ref[:, cols] = a[tm - SUBLANES:, :]
        a1 = _shift_rows(a, prev, 1)
        a2 = _shift_rows(a, prev, 2)
        z = (a2 * cw_ref[0:1, cols] + a1 * cw_ref[1:2, cols] + a * cw_ref[2:3, cols]
             + cb_ref[:, cols])
        u_ref[:, cols] = (z * jax.nn.sigmoid(z) * v).astype(_BF16)
    o_ref[...] = x + _dot(u_ref[...], wo_ref[...])
    for src, dst in zip(cast_in, cast_out):
        dst[...] = src[...].astype(_BF16)


def _slab_rows(rows, n_steps):
    for n_slabs in range(n_steps, 0, -1):
        if n_steps % n_slabs == 0 and rows % n_slabs == 0 and (rows // n_slabs) % (2 * SUBLANES) == 0:
            return rows // n_slabs
    raise ValueError(f"no slab size for {rows} rows in {n_steps} steps")


def _ffn_layer(x2, seq, g, wg, wv, cw, cb, wo, cast=()):
    t, d = x2.shape
    f = wg.shape[1]
    tm = FFN_ROW_TILE
    n_steps = t // tm
    row = pl.BlockSpec((tm, d), lambda i: (i, 0))
    slabs_in, slabs_out, shapes = [], [], []
    for w, layer in cast:
        _, rows, cols = w.shape
        r = _slab_rows(rows, n_steps)
        hold = n_steps // (rows // r)
        slabs_in.append(pl.BlockSpec((None, r, cols),
                                     lambda i, layer=layer, hold=hold: (layer, i // hold, 0)))
        slabs_out.append(pl.BlockSpec((r, cols), lambda i, hold=hold: (i // hold, 0)))
        shapes.append(jax.ShapeDtypeStruct((rows, cols), _BF16))
    outs = pl.pallas_call(
        functools.partial(_ffn_kernel, seq // tm, len(cast)),
        grid=(n_steps,),
        in_specs=[row, _whole((1, d)), _whole((d, f)), _whole((d, f)),
                  _whole((CONV_W, f)), _whole((1, f)), _whole((f, d))] + slabs_in,
        out_specs=[row] + slabs_out,
        out_shape=[jax.ShapeDtypeStruct((t, d), _F32)] + shapes,
        scratch_shapes=[pltpu.VMEM((SUBLANES, f), _F32), pltpu.VMEM((tm, f), _BF16)],
        compiler_params=_params(("arbitrary",)),
        name="ffn_layer",
    )(x2, g, wg, wv, cw, cb, wo, *[w for w, _ in cast])
    return outs[0], list(outs[1:])


def _head_rms_scale(y):
    low = lax.broadcasted_iota(jnp.int32, (1, LANES), 1) < HEAD_DIM
    sq = y * y
    parts = []
    for j in range(y.shape[1] // LANES):
        blk = sq[:, j * LANES:(j + 1) * LANES]
        s_lo = jnp.sum(jnp.where(low, blk, 0.0), axis=-1, keepdims=True)
        s_hi = jnp.sum(jnp.where(low, 0.0, blk), axis=-1, keepdims=True)
        r_lo = lax.rsqrt(s_lo * (1.0 / HEAD_DIM) + EPS)
        r_hi = lax.rsqrt(s_hi * (1.0 / HEAD_DIM) + EPS)
        parts.append(jnp.where(low, r_lo, r_hi))
    return jnp.concatenate(parts, axis=1)


def _qkv_kernel(x_ref, g_ref, w_ref, wvt_ref, qn_ref, kn_ref, q_ref, k_ref, vt_ref):
    x = x_ref[...]
    d = x.shape[1]
    hb = _rmsnorm(x, g_ref[...]).astype(_BF16)
    q = _dot(hb, w_ref[:, 0:d])
    q = q * _head_rms_scale(q) * qn_ref[...]
    q_ref[...] = (q * (HEAD_DIM ** -0.5 * LOG2E)).astype(_BF16)
    k = _dot(hb, w_ref[:, d:2 * d])
    k = k * _head_rms_scale(k) * kn_ref[...]
    k_ref[...] = k.astype(_BF16)
    vt_ref[...] = lax.dot_general(wvt_ref[...], hb, (((1,), (1,)), ((), ())),
                                  preferred_element_type=_F32).astype(_BF16)


def _qkv_layer(x2, g, w, wvt, qn, kn, batch):
    t, d = x2.shape
    tm = ROW_TILE
    tiles_per_seq = t // batch // tm
    row = pl.BlockSpec((tm, d), lambda i: (i, 0))
    col = pl.BlockSpec((None, d, tm), lambda i: (i // tiles_per_seq, 0, i % tiles_per_seq))
    vec = _whole((1, d))
    out = jax.ShapeDtypeStruct((t, d), _BF16)
    return pl.pallas_call(
        _qkv_kernel,
        grid=(t // tm,),
        in_specs=[row, vec, _whole((d, 2 * d)), _whole((d, d)), vec, vec],
        out_specs=[row, row, col],
        out_shape=[out, out, jax.ShapeDtypeStruct((batch, d, t // batch), _BF16)],
        compiler_params=_params(("arbitrary",)),
        name="qkv_layer",
    )(x2, g, w, wvt, qn, kn)


def _attn_window(s, use_prev):
    n_cur = (s + 1) * ATTN_SUB
    n_prev = ATTN_WIN - n_cur
    return (0 if use_prev else n_prev), n_prev, n_cur


def _attn_scores(s, p, use_prev, q_ref, kp_ref, kc_ref, bt_ref, sc_ref):
    tq = q_ref.shape[0]
    j0, n_prev, n_cur = _attn_window(s, use_prev)
    cols = slice(p * LANES, (p + 1) * LANES)
    low_lanes = lax.broadcasted_iota(jnp.int32, (ATTN_SUB, LANES), 1) < HEAD_DIM
    q2 = q_ref[s * ATTN_SUB:(s + 1) * ATTN_SUB, cols]
    zero = jnp.zeros_like(q2)
    qq = jnp.concatenate([jnp.where(low_lanes, q2, zero),
                          jnp.where(low_lanes, zero, q2)], axis=0)
    trans_b = (((1,), (1,)), ((), ()))
    parts = []
    if use_prev:
        parts.append(lax.dot_general(kp_ref[tq - n_prev:tq, cols], qq, trans_b,
                                     preferred_element_type=_F32))
    parts.append(lax.dot_general(kc_ref[0:n_cur, cols], qq, trans_b,
                                 preferred_element_type=_F32))
    sc_ref[s % 2, p, j0:, :] = jnp.concatenate(parts, axis=0) + bt_ref[p, j0:, :]


def _attn_values(s, p, use_prev, vtp_ref, vtc_ref, sc_ref, att_ref):
    tq = att_ref.shape[0]
    j0, n_prev, n_cur = _attn_window(s, use_prev)
    cols = slice(p * LANES, (p + 1) * LANES)
    low_rows = lax.broadcasted_iota(jnp.int32, (LANES, ATTN_SUB), 0) < HEAD_DIM
    sc = sc_ref[s % 2, p, j0:, :]
    e = jnp.exp2(sc - jnp.max(sc, axis=0, keepdims=True)).astype(_BF16)
    ones = jnp.ones((2 * SUBLANES, n_cur), _BF16)
    ot = _dot(jnp.concatenate([vtc_ref[cols, 0:n_cur], ones], axis=0), e[n_prev - j0:, :])
    if use_prev:
        ones = jnp.ones((2 * SUBLANES, n_prev), _BF16)
        ot = ot + _dot(jnp.concatenate([vtp_ref[cols, tq - n_prev:tq], ones], axis=0),
                       e[0:n_prev, :])
    inv = 1.0 / ot[LANES:LANES + 1, :]
    out_t = jnp.where(low_rows, ot[0:LANES, 0:ATTN_SUB] * inv[:, 0:ATTN_SUB],
                      ot[0:LANES, ATTN_SUB:] * inv[:, ATTN_SUB:])
    att_ref[s * ATTN_SUB:(s + 1) * ATTN_SUB, cols] = out_t.T.astype(_BF16)


def _attn_kernel(x_ref, q_ref, kp_ref, kc_ref, vtp_ref, vtc_ref, bt_ref, wo_ref,
                 o_ref, att_ref, sc_ref):
    first = pl.program_id(1) == 0
    tq, d = q_ref.shape
    n_sub = tq // ATTN_SUB
    n_pairs = d // LANES

    def run(use_prev):
        scores = functools.partial(_attn_scores, use_prev=use_prev, q_ref=q_ref,
                                   kp_ref=kp_ref, kc_ref=kc_ref, bt_ref=bt_ref,
                                   sc_ref=sc_ref)
        values = functools.partial(_attn_values, use_prev=use_prev, vtp_ref=vtp_ref,
                                   vtc_ref=vtc_ref, sc_ref=sc_ref, att_ref=att_ref)
        for p in range(n_pairs):
            scores(0, p)
        for s in range(n_sub):
            for p in range(n_pairs):
                if s + 1 < n_sub:
                    scores(s + 1, p)
                values(s, p)

    pl.when(first)(functools.partial(run, False))
    pl.when(jnp.logical_not(first))(functools.partial(run, True))
    o_ref[...] = x_ref[...] + _dot(att_ref[...], wo_ref[...])


def _attn_layer(x3, q3, k3, vt3, bt, wo):
    b, s, d = x3.shape
    tq = ROW_TILE
    assert tq == LEFT_CHUNKS * CHUNK
    cur = pl.BlockSpec((None, tq, d), lambda bi, i: (bi, i, 0))
    prev = pl.BlockSpec((None, tq, d), lambda bi, i: (bi, jnp.maximum(i - 1, 0), 0))
    cur_t = pl.BlockSpec((None, d, tq), lambda bi, i: (bi, 0, i))
    prev_t = pl.BlockSpec((None, d, tq), lambda bi, i: (bi, 0, jnp.maximum(i - 1, 0)))
    n_pairs = d // LANES
    return pl.pallas_call(
        _attn_kernel,
        grid=(b, s // tq),
        in_specs=[cur, cur, prev, cur, prev_t, cur_t,
                  _whole((n_pairs, ATTN_WIN, 2 * ATTN_SUB)), _whole((d, d))],
        out_specs=cur,
        out_shape=jax.ShapeDtypeStruct((b, s, d), _F32),
        scratch_shapes=[pltpu.VMEM((tq, d), _BF16),
                        pltpu.VMEM((2, n_pairs, ATTN_WIN, 2 * ATTN_SUB), _F32)],
        compiler_params=_params(("arbitrary", "arbitrary")),
        name="attn_layer",
    )(x3, q3, k3, k3, vt3, vt3, bt, wo)


BIAS_ROW = 768


def _bias_kernel(c_ref, o_ref):
    kc = lax.broadcasted_iota(jnp.int32, (ATTN_WIN, ATTN_SUB), 0) // CHUNK
    qc = lax.broadcasted_iota(jnp.int32, (ATTN_WIN, ATTN_SUB), 1) // CHUNK
    valid = jnp.logical_and(kc >= qc, kc <= qc + LEFT_CHUNKS)
    halves = []
    for hh in range(2):
        x = jnp.broadcast_to(c_ref[hh], (ATTN_WIN, BIAS_ROW))
        y = pltpu.roll(x, BIAS_ROW - ATTN_WIN + 1, axis=1, stride=1, stride_axis=0)
        halves.append(jnp.where(valid, y[:, :ATTN_SUB] * LOG2E, NEG_BIG))
    o_ref[0] = jnp.concatenate(halves, axis=1)


def _bias_table(rel_table):
    h, n = rel_table.shape
    lo = n - 1 - REL_CLIP - (ATTN_SUB - 1)
    tail = jnp.broadcast_to(rel_table[:, n - 1:n], (h, BIAS_ROW - (n - lo)))
    c = jnp.concatenate([rel_table[:, lo:], tail], axis=1).astype(_F32)[:, None, :]
    return pl.pallas_call(
        _bias_kernel,
        grid=(h // 2,),
        in_specs=[pl.BlockSpec((2, 1, BIAS_ROW), lambda i: (i, 0, 0))],
        out_specs=pl.BlockSpec((1, ATTN_WIN, 2 * ATTN_SUB), lambda i: (i, 0, 0)),
        out_shape=jax.ShapeDtypeStruct((h // 2, ATTN_WIN, 2 * ATTN_SUB), _F32),
        compiler_params=_params(("arbitrary",)),
        name="bias_table",
    )(c)


def kernel(x, mix_norm, ffn_norm, pool_w, pool_b, pool_scale, attn_wqkv, attn_q_norm,
           attn_k_norm, attn_rel_bias, attn_wo, ffn_w_gate, ffn_w_val, ffn_conv_w,
           ffn_conv_b, ffn_w_out):
    b, s, d = x.shape
    depth = mix_norm.shape[0]
    assert all(s % tile == 0 for tile in (ROW_TILE, FFN_ROW_TILE, POOL_ROW_TILE))
    assert d % MXU_COLS == 0 and d == N_HEADS * HEAD_DIM
    assert ffn_w_gate.shape[2] % FFN_COLS == 0
    x2 = x.reshape(b * s, d)
    ffn_b = [w[0].astype(_BF16) for w in (ffn_w_gate, ffn_w_val, ffn_w_out)]
    attn_b = None
    for i in range(depth):
        j = i // 2
        g = mix_norm[i][None, :]
        if i % 2 == 0:
            x2 = _pool_layer(x2, g, pool_w[j].astype(_BF16), pool_b[j][None, :],
                             pool_scale[j][None, :], s)
        else:
            wqkv_b, wo_b = attn_b
            qn = jnp.tile(attn_q_norm[j], N_HEADS)[None, :]
            kn = jnp.tile(attn_k_norm[j], N_HEADS)[None, :]
            q, k, vt = _qkv_layer(x2, g, wqkv_b, wqkv_b[:, 2 * d:].T, qn, kn, b)
            shp = (b, s, d)
            x2 = _attn_layer(x2.reshape(shp), q.reshape(shp), k.reshape(shp), vt,
                             _bias_table(attn_rel_bias[j]), wo_b).reshape(b * s, d)
        pending = []
        if i + 1 < depth:
            pending += [(w, i + 1) for w in (ffn_w_gate, ffn_w_val, ffn_w_out)]
            if (i + 1) % 2 == 1:
                pending += [(w, (i + 1) // 2) for w in (attn_wqkv, attn_wo)]
        x2, done = _ffn_layer(x2, s, ffn_norm[i][None, :], ffn_b[0], ffn_b[1], ffn_conv_w[i],
                              ffn_conv_b[i][None, :], ffn_b[2], cast=pending)
        ffn_b, attn_b = done[:3], done[3:]
    return x2.reshape(b, s, d)
```

```python
import functools

import jax
import jax.numpy as jnp
from jax import lax
from jax.experimental import pallas as pl
from jax.experimental.pallas import tpu as pltpu

EPS = 1e-6
CHUNK = 64
LEFT_CHUNKS = 8
N_HEADS = 16
HEAD_DIM = 64
REL_CLIP = 256
POOL_WINDOWS = (2, 4, 8, 16)
CONV_W = 3
NEG_BIG = -1e30
LOG2E = 1.4426950408889634

SUBLANES = 8
LANES = 128
MXU_COLS = 256

ROW_TILE = 512
FFN_ROW_TILE = 1024
POOL_ROW_TILE = 2048
QKV_ROW_TILE = 1024
POOL_HALO = 16
POOL_BLOCK = 128
ATTN_SUB = 128
ATTN_WIN = ATTN_SUB + LEFT_CHUNKS * CHUNK
FFN_COLS = 256
VMEM_LIMIT = 56 * 1024 * 1024

_BF16 = jnp.bfloat16
_F32 = jnp.float32


def _rmsnorm(x, g):
    ms = jnp.mean(x * x, axis=-1, keepdims=True)
    return x * lax.rsqrt(ms + EPS) * g


def _dot(a, b):
    return jnp.dot(a, b, preferred_element_type=_F32)


def _whole(shape):
    zeros = (0,) * len(shape)
    return pl.BlockSpec(shape, lambda *_: zeros, pipeline_mode=pl.Buffered(1))


def _params(semantics):
    return pltpu.CompilerParams(dimension_semantics=semantics,
                                vmem_limit_bytes=VMEM_LIMIT)


def _pool_kernel(tiles_per_seq, x_ref, g_ref, band_ref, w_ref, b_ref, sc_ref, o_ref,
                 hext_ref, hi_ref, lo_ref):
    tm, d = x_ref.shape
    t_in_seq = pl.program_id(0) % tiles_per_seq

    @pl.when(t_in_seq == 0)
    def _():
        hext_ref[tm:, :] = jnp.zeros((POOL_HALO, d), _F32)

    hext_ref[0:POOL_HALO, :] = hext_ref[tm:tm + POOL_HALO, :]
    hext_ref[POOL_HALO:, :] = _rmsnorm(x_ref[...], g_ref[...])
    hext = hext_ref[...]
    hi = hext.astype(_BF16)
    hi_ref[...] = hi
    lo_ref[...] = (hext - hi.astype(_F32)).astype(_BF16)

    gw = d // len(POOL_WINDOWS)
    blk = POOL_BLOCK
    for rb in range(tm // blk):
        rows = slice(rb * blk, (rb + 1) * blk)
        src = slice(rb * blk, (rb + 1) * blk + POOL_HALO)
        for gi in range(len(POOL_WINDOWS)):
            cols = slice(gi * gw, (gi + 1) * gw)
            band = band_ref[gi]
            o_ref[rows, cols] = _dot(band, hi_ref[src, cols]) + _dot(band, lo_ref[src, cols])
    for rb in range(tm // blk):
        rows = slice(rb * blk, (rb + 1) * blk)
        pos = t_in_seq * tm + rb * blk + lax.broadcasted_iota(jnp.int32, (blk, 1), 0)
        for gi, w in enumerate(POOL_WINDOWS):
            cols = slice(gi * gw, (gi + 1) * gw)
            inv_cnt = 1.0 / jnp.minimum(pos + 1, w).astype(_F32)
            h = hext_ref[rb * blk + POOL_HALO:(rb + 1) * blk + POOL_HALO, cols]
            y = _dot((o_ref[rows, cols] * inv_cnt - h).astype(_BF16), w_ref[gi])
            o_ref[rows, cols] = x_ref[rows, cols] + (y + b_ref[:, cols]) * sc_ref[:, cols]


def _pool_bands():
    i = jnp.arange(POOL_BLOCK)[:, None] + POOL_HALO
    k = jnp.arange(POOL_BLOCK + POOL_HALO)[None, :]
    return jnp.stack([jnp.logical_and(k <= i, k > i - w) for w in POOL_WINDOWS]).astype(_BF16)


def _pool_layer(x2, g, w, b, sc, seq):
    t, d = x2.shape
    tm = POOL_ROW_TILE
    gw = d // len(POOL_WINDOWS)
    n_g = len(POOL_WINDOWS)
    row = pl.BlockSpec((tm, d), lambda i: (i, 0))
    vec = _whole((1, d))
    return pl.pallas_call(
        functools.partial(_pool_kernel, seq // tm),
        grid=(t // tm,),
        in_specs=[row, vec, _whole((n_g, POOL_BLOCK, POOL_BLOCK + POOL_HALO)),
                  _whole((n_g, gw, gw)), vec, vec],
        out_specs=row,
        out_shape=jax.ShapeDtypeStruct((t, d), _F32),
        scratch_shapes=[pltpu.VMEM((POOL_HALO + tm, d), _F32),
                        pltpu.VMEM((POOL_HALO + tm, d), _BF16),
                        pltpu.VMEM((POOL_HALO + tm, d), _BF16)],
        compiler_params=_params(("arbitrary",)),
        name="pool_layer",
    )(x2, g, _pool_bands(), w, b, sc)


def _shift_rows(a, prev, k):
    rolled = pltpu.roll(a, k, axis=0)
    prolled = pltpu.roll(prev, k, axis=0)
    rows = lax.broadcasted_iota(jnp.int32, prev.shape, 0)
    head = jnp.where(rows < k, prolled, rolled[:SUBLANES, :])
    return jnp.concatenate([head, rolled[SUBLANES:, :]], axis=0)


def _ffn_kernel(tiles_per_seq, n_cast, *refs):
    x_ref, g_ref, wg_ref, wv_ref, cw_ref, cb_ref, wo_ref = refs[:7]
    cast_in = refs[7:7 + n_cast]
    o_ref = refs[7 + n_cast]
    cast_out = refs[8 + n_cast:8 + 2 * n_cast]
    carry_ref, u_ref = refs[8 + 2 * n_cast:]

    @pl.when(pl.program_id(0) % tiles_per_seq == 0)
    def _():
        carry_ref[...] = jnp.zeros_like(carry_ref)

    x = x_ref[...]
    tm = x.shape[0]
    hb = _rmsnorm(x, g_ref[...]).astype(_BF16)
    f = wg_ref.shape[1]
    for c in range(f // FFN_COLS):
        cols = slice(c * FFN_COLS, (c + 1) * FFN_COLS)
        a = _dot(hb, wg_ref[:, cols])
        v = _dot(hb, wv_ref[:, cols])
        prev = carry_ref[:, cols]
        carry_ref[:, cols] = a[tm - SUBLANES:, :]
        a1 = _shift_rows(a, prev, 1)
        a2 = _shift_rows(a, prev, 2)
        z = (a2 * cw_ref[0:1, cols] + a1 * cw_ref[1:2, cols] + a * cw_ref[2:3, cols]
             + cb_ref[:, cols])
        u_ref[:, cols] = (z * jax.nn.sigmoid(z) * v).astype(_BF16)
    o_ref[...] = x + _dot(u_ref[...], wo_ref[...])
    for src, dst in zip(cast_in, cast_out):
        dst[...] = src[...].astype(_BF16)


def _slab_rows(rows, n_steps):
    for n_slabs in range(n_steps, 0, -1):
        if n_steps % n_slabs == 0 and rows % n_slabs == 0 and (rows // n_slabs) % (2 * SUBLANES) == 0:
            return rows // n_slabs
    raise ValueError(f"no slab size for {rows} rows in {n_steps} steps")


def _ffn_layer(x2, seq, g, wg, wv, cw, cb, wo, cast=()):
    t, d = x2.shape
    f = wg.shape[1]
    tm = FFN_ROW_TILE
    n_steps = t // tm
    row = pl.BlockSpec((tm, d), lambda i: (i, 0))
    slabs_in, slabs_out, shapes = [], [], []
    for w, layer in cast:
        _, rows, cols = w.shape
        r = _slab_rows(rows, n_steps)
        hold = n_steps // (rows // r)
        slabs_in.append(pl.BlockSpec((None, r, cols),
                                     lambda i, layer=layer, hold=hold: (layer, i // hold, 0)))
        slabs_out.append(pl.BlockSpec((r, cols), lambda i, hold=hold: (i // hold, 0)))
        shapes.append(jax.ShapeDtypeStruct((rows, cols), _BF16))
    outs = pl.pallas_call(
        functools.partial(_ffn_kernel, seq // tm, len(cast)),
        grid=(n_steps,),
        in_specs=[row, _whole((1, d)), _whole((d, f)), _whole((d, f)),
                  _whole((CONV_W, f)), _whole((1, f)), _whole((f, d))] + slabs_in,
        out_specs=[row] + slabs_out,
        out_shape=[jax.ShapeDtypeStruct((t, d), _F32)] + shapes,
        scratch_shapes=[pltpu.VMEM((SUBLANES, f), _F32), pltpu.VMEM((tm, f), _BF16)],
        compiler_params=_params(("arbitrary",)),
        name="ffn_layer",
    )(x2, g, wg, wv, cw, cb, wo, *[w for w, _ in cast])
    return outs[0], list(outs[1:])


def _head_rms_scale(y):
    low = lax.broadcasted_iota(jnp.int32, (1, LANES), 1) < HEAD_DIM
    sq = y * y
    parts = []
    for j in range(y.shape[1] // LANES):
        blk = sq[:, j * LANES:(j + 1) * LANES]
        s_lo = jnp.sum(jnp.where(low, blk, 0.0), axis=-1, keepdims=True)
        s_hi = jnp.sum(jnp.where(low, 0.0, blk), axis=-1, keepdims=True)
        r_lo = lax.rsqrt(s_lo * (1.0 / HEAD_DIM) + EPS)
        r_hi = lax.rsqrt(s_hi * (1.0 / HEAD_DIM) + EPS)
        parts.append(jnp.where(low, r_lo, r_hi))
    return jnp.concatenate(parts, axis=1)


def _qkv_kernel(x_ref, g_ref, w_ref, wvt_ref, qn_ref, kn_ref, q_ref, k_ref, vt_ref):
    x = x_ref[...]
    d = x.shape[1]
    hb = _rmsnorm(x, g_ref[...]).astype(_BF16)
    q = _dot(hb, w_ref[:, 0:d])
    q = q * _head_rms_scale(q) * qn_ref[...]
    q_ref[...] = (q * (HEAD_DIM ** -0.5 * LOG2E)).astype(_BF16)
    k = _dot(hb, w_ref[:, d:2 * d])
    k = k * _head_rms_scale(k) * kn_ref[...]
    k_ref[...] = k.astype(_BF16)
    vt_ref[...] = lax.dot_general(wvt_ref[...], hb, (((1,), (1,)), ((), ())),
                                  preferred_element_type=_F32).astype(_BF16)


def _qkv_layer(x2, g, w, wvt, qn, kn, batch):
    t, d = x2.shape
    tm = QKV_ROW_TILE
    tiles_per_seq = t // batch // tm
    row = pl.BlockSpec((tm, d), lambda i: (i, 0))
    col = pl.BlockSpec((None, d, tm), lambda i: (i // tiles_per_seq, 0, i % tiles_per_seq))
    vec = _whole((1, d))
    out = jax.ShapeDtypeStruct((t, d), _BF16)
    return pl.pallas_call(
        _qkv_kernel,
        grid=(t // tm,),
        in_specs=[row, vec, _whole((d, 2 * d)), _whole((d, d)), vec, vec],
        out_specs=[row, row, col],
        out_shape=[out, out, jax.ShapeDtypeStruct((batch, d, t // batch), _BF16)],
        compiler_params=_params(("arbitrary",)),
        name="qkv_layer",
    )(x2, g, w, wvt, qn, kn)


def _attn_window(s, use_prev):
    n_cur = (s + 1) * ATTN_SUB
    n_prev = ATTN_WIN - n_cur
    return (0 if use_prev else n_prev), n_prev, n_cur


def _attn_scores(s, p, use_prev, q_ref, kp_ref, kc_ref, bt_ref, sc_ref):
    tq = q_ref.shape[0]
    j0, n_prev, n_cur = _attn_window(s, use_prev)
    cols = slice(p * LANES, (p + 1) * LANES)
    low_lanes = lax.broadcasted_iota(jnp.int32, (ATTN_SUB, LANES), 1) < HEAD_DIM
    q2 = q_ref[s * ATTN_SUB:(s + 1) * ATTN_SUB, cols]
    zero = jnp.zeros_like(q2)
    qq = jnp.concatenate([jnp.where(low_lanes, q2, zero),
                          jnp.where(low_lanes, zero, q2)], axis=0)
    trans_b = (((1,), (1,)), ((), ()))
    parts = []
    if use_prev:
        parts.append(lax.dot_general(kp_ref[tq - n_prev:tq, cols], qq, trans_b,
                                     preferred_element_type=_F32))
    parts.append(lax.dot_general(kc_ref[0:n_cur, cols], qq, trans_b,
                                 preferred_element_type=_F32))
    sc_ref[s % 2, p, j0:, :] = jnp.concatenate(parts, axis=0) + bt_ref[p, j0:, :]


def _attn_values(s, p, use_prev, vtp_ref, vtc_ref, sc_ref, att_ref):
    tq = att_ref.shape[0]
    j0, n_prev, n_cur = _attn_window(s, use_prev)
    cols = slice(p * LANES, (p + 1) * LANES)
    low_rows = lax.broadcasted_iota(jnp.int32, (LANES, ATTN_SUB), 0) < HEAD_DIM
    sc = sc_ref[s % 2, p, j0:, :]
    e = jnp.exp2(sc - jnp.max(sc, axis=0, keepdims=True)).astype(_BF16)
    ones = jnp.ones((2 * SUBLANES, n_cur), _BF16)
    ot = _dot(jnp.concatenate([vtc_ref[cols, 0:n_cur], ones], axis=0), e[n_prev - j0:, :])
    if use_prev:
        ones = jnp.ones((2 * SUBLANES, n_prev), _BF16)
        ot = ot + _dot(jnp.concatenate([vtp_ref[cols, tq - n_prev:tq], ones], axis=0),
                       e[0:n_prev, :])
    inv = 1.0 / ot[LANES:LANES + 1, :]
    out_t = jnp.where(low_rows, ot[0:LANES, 0:ATTN_SUB] * inv[:, 0:ATTN_SUB],
                      ot[0:LANES, ATTN_SUB:] * inv[:, ATTN_SUB:])
    att_ref[s * ATTN_SUB:(s + 1) * ATTN_SUB, cols] = out_t.T.astype(_BF16)


def _attn_kernel(x_ref, q_ref, kp_ref, kc_ref, vtp_ref, vtc_ref, bt_ref, wo_ref,
                 o_ref, att_ref, sc_ref):
    first = pl.program_id(1) == 0
    tq, d = q_ref.shape
    n_sub = tq // ATTN_SUB
    n_pairs = d // LANES

    def run(use_prev):
        scores = functools.partial(_attn_scores, use_prev=use_prev, q_ref=q_ref,
                                   kp_ref=kp_ref, kc_ref=kc_ref, bt_ref=bt_ref,
                                   sc_ref=sc_ref)
        values = functools.partial(_attn_values, use_prev=use_prev, vtp_ref=vtp_ref,
                                   vtc_ref=vtc_ref, sc_ref=sc_ref, att_ref=att_ref)
        for p in range(n_pairs):
            scores(0, p)
        for s in range(n_sub):
            for p in range(n_pairs):
                if s + 1 < n_sub:
                    scores(s + 1, p)
                values(s, p)

    pl.when(first)(functools.partial(run, False))
    pl.when(jnp.logical_not(first))(functools.partial(run, True))
    o_ref[...] = x_ref[...] + _dot(att_ref[...], wo_ref[...])


def _attn_layer(x3, q3, k3, vt3, bt, layer, wo):
    b, s, d = x3.shape
    tq = ROW_TILE
    assert tq == LEFT_CHUNKS * CHUNK
    cur = pl.BlockSpec((None, tq, d), lambda bi, i: (bi, i, 0))
    prev = pl.BlockSpec((None, tq, d), lambda bi, i: (bi, jnp.maximum(i - 1, 0), 0))
    cur_t = pl.BlockSpec((None, d, tq), lambda bi, i: (bi, 0, i))
    prev_t = pl.BlockSpec((None, d, tq), lambda bi, i: (bi, 0, jnp.maximum(i - 1, 0)))
    n_pairs = d // LANES
    table = pl.BlockSpec((n_pairs, ATTN_WIN, 2 * ATTN_SUB), lambda bi, i: (layer, 0, 0),
                         pipeline_mode=pl.Buffered(1))
    return pl.pallas_call(
        _attn_kernel,
        grid=(b, s // tq),
        in_specs=[cur, cur, prev, cur, prev_t, cur_t, table, _whole((d, d))],
        out_specs=cur,
        out_shape=jax.ShapeDtypeStruct((b, s, d), _F32),
        scratch_shapes=[pltpu.VMEM((tq, d), _BF16),
                        pltpu.VMEM((2, n_pairs, ATTN_WIN, 2 * ATTN_SUB), _F32)],
        compiler_params=_params(("arbitrary", "arbitrary")),
        name="attn_layer",
    )(x3, q3, k3, k3, vt3, vt3, bt, wo)


BIAS_ROW = 768


def _bias_kernel(c_ref, o_ref):
    kc = lax.broadcasted_iota(jnp.int32, (ATTN_WIN, ATTN_SUB), 0) // CHUNK
    qc = lax.broadcasted_iota(jnp.int32, (ATTN_WIN, ATTN_SUB), 1) // CHUNK
    valid = jnp.logical_and(kc >= qc, kc <= qc + LEFT_CHUNKS)
    halves = []
    for hh in range(2):
        x = jnp.broadcast_to(c_ref[hh], (ATTN_WIN, BIAS_ROW))
        y = pltpu.roll(x, BIAS_ROW - ATTN_WIN + 1, axis=1, stride=1, stride_axis=0)
        halves.append(jnp.where(valid, y[:, :ATTN_SUB] * LOG2E, NEG_BIG))
    o_ref[0] = jnp.concatenate(halves, axis=1)


def _bias_table(rel_table):
    h, n = rel_table.shape
    lo = n - 1 - REL_CLIP - (ATTN_SUB - 1)
    tail = jnp.broadcast_to(rel_table[:, n - 1:n], (h, BIAS_ROW - (n - lo)))
    c = jnp.concatenate([rel_table[:, lo:], tail], axis=1).astype(_F32)[:, None, :]
    return pl.pallas_call(
        _bias_kernel,
        grid=(h // 2,),
        in_specs=[pl.BlockSpec((2, 1, BIAS_ROW), lambda i: (i, 0, 0))],
        out_specs=pl.BlockSpec((1, ATTN_WIN, 2 * ATTN_SUB), lambda i: (i, 0, 0)),
        out_shape=jax.ShapeDtypeStruct((h // 2, ATTN_WIN, 2 * ATTN_SUB), _F32),
        compiler_params=_params(("arbitrary",)),
        name="bias_table",
    )(c)


def kernel(x, mix_norm, ffn_norm, pool_w, pool_b, pool_scale, attn_wqkv, attn_q_norm,
           attn_k_norm, attn_rel_bias, attn_wo, ffn_w_gate, ffn_w_val, ffn_conv_w,
           ffn_conv_b, ffn_w_out):
    b, s, d = x.shape
    depth = mix_norm.shape[0]
    assert all(s % tile == 0
               for tile in (ROW_TILE, FFN_ROW_TILE, POOL_ROW_TILE, QKV_ROW_TILE))
    assert d % MXU_COLS == 0 and d == N_HEADS * HEAD_DIM
    assert ffn_w_gate.shape[2] % FFN_COLS == 0
    x2 = x.reshape(b * s, d)
    ffn_b = [w[0].astype(_BF16) for w in (ffn_w_gate, ffn_w_val, ffn_w_out)]
    attn_b = None
    bias_tables = _bias_table(attn_rel_bias.reshape(-1, attn_rel_bias.shape[-1]))
    for i in range(depth):
        j = i // 2
        g = mix_norm[i][None, :]
        if i % 2 == 0:
            x2 = _pool_layer(x2, g, pool_w[j].astype(_BF16), pool_b[j][None, :],
                             pool_scale[j][None, :], s)
        else:
            wqkv_b, wo_b = attn_b
            qn = jnp.tile(attn_q_norm[j], N_HEADS)[None, :]
            kn = jnp.tile(attn_k_norm[j], N_HEADS)[None, :]
            q, k, vt = _qkv_layer(x2, g, wqkv_b, wqkv_b[:, 2 * d:].T, qn, kn, b)
            shp = (b, s, d)
            x2 = _attn_layer(x2.reshape(shp), q.reshape(shp), k.reshape(shp), vt,
                             bias_tables, j, wo_b).reshape(b * s, d)
        pending = []
        if i + 1 < depth:
            pending += [(w, i + 1) for w in (ffn_w_gate, ffn_w_val, ffn_w_out)]
            if (i + 1) % 2 == 1:
                pending += [(w, (i + 1) // 2) for w in (attn_wqkv, attn_wo)]
        x2, done = _ffn_layer(x2, s, ffn_norm[i][None, :], ffn_b[0], ffn_b[1], ffn_conv_w[i],
                              ffn_conv_b[i][None, :], ffn_b[2], cast=pending)
        ffn_b, attn_b = done[:3], done[3:]
    return x2.reshape(b, s, d)
```

```python
import functools

import jax
import jax.numpy as jnp
from jax import lax
from jax.experimental import pallas as pl
from jax.experimental.pallas import tpu as pltpu

EPS = 1e-6
CHUNK = 64
LEFT_CHUNKS = 8
N_HEADS = 16
HEAD_DIM = 64
REL_CLIP = 256
POOL_WINDOWS = (2, 4, 8, 16)
CONV_W = 3
NEG_BIG = -1e30
LOG2E = 1.4426950408889634

SUBLANES = 8
LANES = 128
MXU_COLS = 256

ROW_TILE = 512
FFN_ROW_TILE = 1024
POOL_ROW_TILE = 2048
QKV_ROW_TILE = 1024
POOL_HALO = 16
POOL_BLOCK = 128
ATTN_SUB = 128
ATTN_WIN = ATTN_SUB + LEFT_CHUNKS * CHUNK
FFN_COLS = 256
VMEM_LIMIT = 56 * 1024 * 1024

_BF16 = jnp.bfloat16
_F32 = jnp.float32


def _rmsnorm(x, g):
    ms = jnp.mean(x * x, axis=-1, keepdims=True)
    return x * lax.rsqrt(ms + EPS) * g


def _dot(a, b):
    return jnp.dot(a, b, preferred_element_type=_F32)


def _whole(shape):
    zeros = (0,) * len(shape)
    return pl.BlockSpec(shape, lambda *_: zeros, pipeline_mode=pl.Buffered(1))


def _params(semantics):
    return pltpu.CompilerParams(dimension_semantics=semantics,
                                vmem_limit_bytes=VMEM_LIMIT)


def _pool_kernel(tiles_per_seq, x_ref, g_ref, band_ref, w_ref, b_ref, sc_ref, o_ref,
                 hext_ref, hi_ref, lo_ref):
    tm, d = x_ref.shape
    t_in_seq = pl.program_id(0) % tiles_per_seq

    @pl.when(t_in_seq == 0)
    def _():
        hext_ref[tm:, :] = jnp.zeros((POOL_HALO, d), _F32)

    hext_ref[0:POOL_HALO, :] = hext_ref[tm:tm + POOL_HALO, :]
    hext_ref[POOL_HALO:, :] = _rmsnorm(x_ref[...], g_ref[...])
    hext = hext_ref[...]
    hi = hext.astype(_BF16)
    hi_ref[...] = hi
    lo_ref[...] = (hext - hi.astype(_F32)).astype(_BF16)

    gw = d // len(POOL_WINDOWS)
    blk = POOL_BLOCK
    for rb in range(tm // blk):
        rows = slice(rb * blk, (rb + 1) * blk)
        src = slice(rb * blk, (rb + 1) * blk + POOL_HALO)
        for gi in range(len(POOL_WINDOWS)):
            cols = slice(gi * gw, (gi + 1) * gw)
            band = band_ref[gi]
            o_ref[rows, cols] = _dot(band, hi_ref[src, cols]) + _dot(band, lo_ref[src, cols])
    for rb in range(tm // blk):
        rows = slice(rb * blk, (rb + 1) * blk)
        pos = t_in_seq * tm + rb * blk + lax.broadcasted_iota(jnp.int32, (blk, 1), 0)
        for gi, w in enumerate(POOL_WINDOWS):
            cols = slice(gi * gw, (gi + 1) * gw)
            inv_cnt = 1.0 / jnp.minimum(pos + 1, w).astype(_F32)
            h = hext_ref[rb * blk + POOL_HALO:(rb + 1) * blk + POOL_HALO, cols]
            y = _dot((o_ref[rows, cols] * inv_cnt - h).astype(_BF16), w_ref[gi])
            o_ref[rows, cols] = x_ref[rows, cols] + (y + b_ref[:, cols]) * sc_ref[:, cols]


def _pool_bands():
    i = jnp.arange(POOL_BLOCK)[:, None] + POOL_HALO
    k = jnp.arange(POOL_BLOCK + POOL_HALO)[None, :]
    return jnp.stack([jnp.logical_and(k <= i, k > i - w) for w in POOL_WINDOWS]).astype(_BF16)


def _pool_layer(x2, g, w, b, sc, seq):
    t, d = x2.shape
    tm = POOL_ROW_TILE
    gw = d // len(POOL_WINDOWS)
    n_g = len(POOL_WINDOWS)
    row = pl.BlockSpec((tm, d), lambda i: (i, 0))
    vec = _whole((1, d))
    return pl.pallas_call(
        functools.partial(_pool_kernel, seq // tm),
        grid=(t // tm,),
        in_specs=[row, vec, _whole((n_g, POOL_BLOCK, POOL_BLOCK + POOL_HALO)),
                  _whole((n_g, gw, gw)), vec, vec],
        out_specs=row,
        out_shape=jax.ShapeDtypeStruct((t, d), _F32),
        scratch_shapes=[pltpu.VMEM((POOL_HALO + tm, d), _F32),
                        pltpu.VMEM((POOL_HALO + tm, d), _BF16),
                        pltpu.VMEM((POOL_HALO + tm, d), _BF16)],
        compiler_params=_params(("arbitrary",)),
        name="pool_layer",
    )(x2, g, _pool_bands(), w, b, sc)


def _shift_rows(a, prev, k):
    rolled = pltpu.roll(a, k, axis=0)
    prolled = pltpu.roll(prev, k, axis=0)
    rows = lax.broadcasted_iota(jnp.int32, prev.shape, 0)
    head = jnp.where(rows < k, prolled, rolled[:SUBLANES, :])
    return jnp.concatenate([head, rolled[SUBLANES:, :]], axis=0)


def _ffn_kernel(tiles_per_seq, n_cast, *refs):
    x_ref, g_ref, wg_ref, wv_ref, cw_ref, cb_ref, wo_ref = refs[:7]
    cast_in = refs[7:7 + n_cast]
    o_ref = refs[7 + n_cast]
    cast_out = refs[8 + n_cast:8 + 2 * n_cast]
    carry_ref, u_ref = refs[8 + 2 * n_cast:]

    @pl.when(pl.program_id(0) % tiles_per_seq == 0)
    def _():
        carry_ref[...] = jnp.zeros_like(carry_ref)

    x = x_ref[...]
    tm = x.shape[0]
    hb = _rmsnorm(x, g_ref[...]).astype(_BF16)
    f = wg_ref.shape[1]
    for c in range(f // FFN_COLS):
        cols = slice(c * FFN_COLS, (c + 1) * FFN_COLS)
        a = _dot(hb, wg_ref[:, cols])
        v = _dot(hb, wv_ref[:, cols])
        prev = carry_ref[:, cols]
        carry_ref[:, cols] = a[tm - SUBLANES:, :]
        a1 = _shift_rows(a, prev, 1)
        a2 = _shift_rows(a, prev, 2)
        z = (a2 * cw_ref[0:1, cols] + a1 * cw_ref[1:2, cols] + a * cw_ref[2:3, cols]
             + cb_ref[:, cols])
        u_ref[:, cols] = (z * jax.nn.sigmoid(z) * v).astype(_BF16)
    o_ref[...] = x + _dot(u_ref[...], wo_ref[...])
    for src, dst in zip(cast_in, cast_out):
        dst[...] = src[...].astype(_BF16)


def _slab_rows(rows, n_steps):
    for n_slabs in range(n_steps, 0, -1):
        if n_steps % n_slabs == 0 and rows % n_slabs == 0 and (rows // n_slabs) % (2 * SUBLANES) == 0:
            return rows // n_slabs
    raise ValueError(f"no slab size for {rows} rows in {n_steps} steps")


def _ffn_layer(x2, seq, g, wg, wv, cw, cb, wo, cast=()):
    t, d = x2.shape
    f = wg.shape[1]
    tm = FFN_ROW_TILE
    n_steps = t // tm
    row = pl.BlockSpec((tm, d), lambda i: (i, 0))
    slabs_in, slabs_out, shapes = [], [], []
    for w, layer in cast:
        _, rows, cols = w.shape
        r = _slab_rows(rows, n_steps)
        hold = n_steps // (rows // r)
        slabs_in.append(pl.BlockSpec((None, r, cols),
                                     lambda i, layer=layer, hold=hold: (layer, i // hold, 0)))
        slabs_out.append(pl.BlockSpec((r, cols), lambda i, hold=hold: (i // hold, 0)))
        shapes.append(jax.ShapeDtypeStruct((rows, cols), _BF16))
    outs = pl.pallas_call(
        functools.partial(_ffn_kernel, seq // tm, len(cast)),
        grid=(n_steps,),
        in_specs=[row, _whole((1, d)), _whole((d, f)), _whole((d, f)),
                  _whole((CONV_W, f)), _whole((1, f)), _whole((f, d))] + slabs_in,
        out_specs=[row] + slabs_out,
        out_shape=[jax.ShapeDtypeStruct((t, d), _F32)] + shapes,
        scratch_shapes=[pltpu.VMEM((SUBLANES, f), _F32), pltpu.VMEM((tm, f), _BF16)],
        compiler_params=_params(("arbitrary",)),
        name="ffn_layer",
    )(x2, g, wg, wv, cw, cb, wo, *[w for w, _ in cast])
    return outs[0], list(outs[1:])


def _head_rms_scale(y):
    low = lax.broadcasted_iota(jnp.int32, (1, LANES), 1) < HEAD_DIM
    sq = y * y
    parts = []
    for j in range(y.shape[1] // LANES):
        blk = sq[:, j * LANES:(j + 1) * LANES]
        s_lo = jnp.sum(jnp.where(low, blk, 0.0), axis=-1, keepdims=True)
        s_hi = jnp.sum(jnp.where(low, 0.0, blk), axis=-1, keepdims=True)
        r_lo = lax.rsqrt(s_lo * (1.0 / HEAD_DIM) + EPS)
        r_hi = lax.rsqrt(s_hi * (1.0 / HEAD_DIM) + EPS)
        parts.append(jnp.where(low, r_lo, r_hi))
    return jnp.concatenate(parts, axis=1)


def _qkv_kernel(x_ref, g_ref, w_ref, wvt_ref, qn_ref, kn_ref, q_ref, k_ref, vt_ref):
    x = x_ref[...]
    d = x.shape[1]
    hb = _rmsnorm(x, g_ref[...]).astype(_BF16)
    q = _dot(hb, w_ref[:, 0:d])
    q = q * _head_rms_scale(q) * qn_ref[...]
    q_ref[...] = (q * (HEAD_DIM ** -0.5 * LOG2E)).astype(_BF16)
    k = _dot(hb, w_ref[:, d:2 * d])
    k = k * _head_rms_scale(k) * kn_ref[...]
    k_ref[...] = k.astype(_BF16)
    vt = lax.dot_general(wvt_ref[...], hb, (((1,), (1,)), ((), ())),
                         preferred_element_type=_F32).astype(_BF16)
    for j in range(vt_ref.shape[0]):
        vt_ref[j] = vt[:, j * ROW_TILE:(j + 1) * ROW_TILE]


def _qkv_layer(x2, g, w, wvt, qn, kn, batch):
    t, d = x2.shape
    tm = QKV_ROW_TILE
    tiles_per_seq = t // batch // tm
    row = pl.BlockSpec((tm, d), lambda i: (i, 0))
    slabs = tm // ROW_TILE
    col = pl.BlockSpec((None, slabs, d, ROW_TILE),
                       lambda i: (i // tiles_per_seq, i % tiles_per_seq, 0, 0))
    vec = _whole((1, d))
    out = jax.ShapeDtypeStruct((t, d), _BF16)
    vt_shape = (batch, t // batch // ROW_TILE, d, ROW_TILE)
    return pl.pallas_call(
        _qkv_kernel,
        grid=(t // tm,),
        in_specs=[row, vec, _whole((d, 2 * d)), _whole((d, d)), vec, vec],
        out_specs=[row, row, col],
        out_shape=[out, out, jax.ShapeDtypeStruct(vt_shape, _BF16)],
        compiler_params=_params(("arbitrary",)),
        name="qkv_layer",
    )(x2, g, w, wvt, qn, kn)


def _attn_window(s, use_prev):
    n_cur = (s + 1) * ATTN_SUB
    n_prev = ATTN_WIN - n_cur
    return (0 if use_prev else n_prev), n_prev, n_cur


def _attn_scores(s, p, use_prev, q_ref, kp_ref, kc_ref, bt_ref, sc_ref):
    tq = q_ref.shape[0]
    j0, n_prev, n_cur = _attn_window(s, use_prev)
    cols = slice(p * LANES, (p + 1) * LANES)
    low_lanes = lax.broadcasted_iota(jnp.int32, (ATTN_SUB, LANES), 1) < HEAD_DIM
    q2 = q_ref[s * ATTN_SUB:(s + 1) * ATTN_SUB, cols]
    zero = jnp.zeros_like(q2)
    qq = jnp.concatenate([jnp.where(low_lanes, q2, zero),
                          jnp.where(low_lanes, zero, q2)], axis=0)
    trans_b = (((1,), (1,)), ((), ()))
    parts = []
    if use_prev:
        parts.append(lax.dot_general(kp_ref[tq - n_prev:tq, cols], qq, trans_b,
                                     preferred_element_type=_F32))
    parts.append(lax.dot_general(kc_ref[0:n_cur, cols], qq, trans_b,
                                 preferred_element_type=_F32))
    sc_ref[s % 2, p, j0:, :] = jnp.concatenate(parts, axis=0) + bt_ref[p, j0:, :]


def _attn_values(s, p, use_prev, vtp_ref, vtc_ref, sc_ref, att_ref):
    tq = att_ref.shape[0]
    j0, n_prev, n_cur = _attn_window(s, use_prev)
    cols = slice(p * LANES, (p + 1) * LANES)
    low_rows = lax.broadcasted_iota(jnp.int32, (LANES, ATTN_SUB), 0) < HEAD_DIM
    sc = sc_ref[s % 2, p, j0:, :]
    e = jnp.exp2(sc - jnp.max(sc, axis=0, keepdims=True)).astype(_BF16)
    ones = jnp.ones((2 * SUBLANES, n_cur), _BF16)
    ot = _dot(jnp.concatenate([vtc_ref[cols, 0:n_cur], ones], axis=0), e[n_prev - j0:, :])
    if use_prev:
        ones = jnp.ones((2 * SUBLANES, n_prev), _BF16)
        ot = ot + _dot(jnp.concatenate([vtp_ref[cols, tq - n_prev:tq], ones], axis=0),
                       e[0:n_prev, :])
    inv = 1.0 / ot[LANES:LANES + 1, :]
    out_t = jnp.where(low_rows, ot[0:LANES, 0:ATTN_SUB] * inv[:, 0:ATTN_SUB],
                      ot[0:LANES, ATTN_SUB:] * inv[:, ATTN_SUB:])
    att_ref[s * ATTN_SUB:(s + 1) * ATTN_SUB, cols] = out_t.T.astype(_BF16)


def _attn_kernel(x_ref, q_ref, kp_ref, kc_ref, vtp_ref, vtc_ref, bt_ref, wo_ref,
                 o_ref, att_ref, sc_ref):
    first = pl.program_id(1) == 0
    tq, d = q_ref.shape
    n_sub = tq // ATTN_SUB
    n_pairs = d // LANES

    def run(use_prev):
        scores = functools.partial(_attn_scores, use_prev=use_prev, q_ref=q_ref,
                                   kp_ref=kp_ref, kc_ref=kc_ref, bt_ref=bt_ref,
                                   sc_ref=sc_ref)
        values = functools.partial(_attn_values, use_prev=use_prev, vtp_ref=vtp_ref,
                                   vtc_ref=vtc_ref, sc_ref=sc_ref, att_ref=att_ref)
        for p in range(n_pairs):
            scores(0, p)
        for s in range(n_sub):
            for p in range(n_pairs):
                if s + 1 < n_sub:
                    scores(s + 1, p)
                values(s, p)

    pl.when(first)(functools.partial(run, False))
    pl.when(jnp.logical_not(first))(functools.partial(run, True))
    o_ref[...] = x_ref[...] + _dot(att_ref[...], wo_ref[...])


def _attn_layer(x3, q3, k3, vt3, bt, layer, wo):
    b, s, d = x3.shape
    tq = ROW_TILE
    assert tq == LEFT_CHUNKS * CHUNK
    cur = pl.BlockSpec((None, tq, d), lambda bi, i: (bi, i, 0))
    prev = pl.BlockSpec((None, tq, d), lambda bi, i: (bi, jnp.maximum(i - 1, 0), 0))
    cur_t = pl.BlockSpec((None, None, d, tq), lambda bi, i: (bi, i, 0, 0))
    prev_t = pl.BlockSpec((None, None, d, tq),
                          lambda bi, i: (bi, jnp.maximum(i - 1, 0), 0, 0))
    n_pairs = d // LANES
    table = pl.BlockSpec((n_pairs, ATTN_WIN, 2 * ATTN_SUB), lambda bi, i: (layer, 0, 0),
                         pipeline_mode=pl.Buffered(1))
    return pl.pallas_call(
        _attn_kernel,
        grid=(b, s // tq),
        in_specs=[cur, cur, prev, cur, prev_t, cur_t, table, _whole((d, d))],
        out_specs=cur,
        out_shape=jax.ShapeDtypeStruct((b, s, d), _F32),
        scratch_shapes=[pltpu.VMEM((tq, d), _BF16),
                        pltpu.VMEM((2, n_pairs, ATTN_WIN, 2 * ATTN_SUB), _F32)],
        compiler_params=_params(("arbitrary", "arbitrary")),
        name="attn_layer",
    )(x3, q3, k3, k3, vt3, vt3, bt, wo)


BIAS_ROW = 768


def _bias_kernel(c_ref, o_ref):
    kc = lax.broadcasted_iota(jnp.int32, (ATTN_WIN, ATTN_SUB), 0) // CHUNK
    qc = lax.broadcasted_iota(jnp.int32, (ATTN_WIN, ATTN_SUB), 1) // CHUNK
    valid = jnp.logical_and(kc >= qc, kc <= qc + LEFT_CHUNKS)
    halves = []
    for hh in range(2):
        x = jnp.broadcast_to(c_ref[hh], (ATTN_WIN, BIAS_ROW))
        y = pltpu.roll(x, BIAS_ROW - ATTN_WIN + 1, axis=1, stride=1, stride_axis=0)
        halves.append(jnp.where(valid, y[:, :ATTN_SUB] * LOG2E, NEG_BIG))
    o_ref[0] = jnp.concatenate(halves, axis=1)


def _bias_table(rel_table):
    h, n = rel_table.shape
    lo = n - 1 - REL_CLIP - (ATTN_SUB - 1)
    tail = jnp.broadcast_to(rel_table[:, n - 1:n], (h, BIAS_ROW - (n - lo)))
    c = jnp.concatenate([rel_table[:, lo:], tail], axis=1).astype(_F32)[:, None, :]
    return pl.pallas_call(
        _bias_kernel,
        grid=(h // 2,),
        in_specs=[pl.BlockSpec((2, 1, BIAS_ROW), lambda i: (i, 0, 0))],
        out_specs=pl.BlockSpec((1, ATTN_WIN, 2 * ATTN_SUB), lambda i: (i, 0, 0)),
        out_shape=jax.ShapeDtypeStruct((h // 2, ATTN_WIN, 2 * ATTN_SUB), _F32),
        compiler_params=_params(("arbitrary",)),
        name="bias_table",
    )(c)


def kernel(x, mix_norm, ffn_norm, pool_w, pool_b, pool_scale, attn_wqkv, attn_q_norm,
           attn_k_norm, attn_rel_bias, attn_wo, ffn_w_gate, ffn_w_val, ffn_conv_w,
           ffn_conv_b, ffn_w_out):
    b, s, d = x.shape
    depth = mix_norm.shape[0]
    assert all(s % tile == 0
               for tile in (ROW_TILE, FFN_ROW_TILE, POOL_ROW_TILE, QKV_ROW_TILE))
    assert d % MXU_COLS == 0 and d == N_HEADS * HEAD_DIM
    assert ffn_w_gate.shape[2] % FFN_COLS == 0
    x2 = x.reshape(b * s, d)
    ffn_b = [w[0].astype(_BF16) for w in (ffn_w_gate, ffn_w_val, ffn_w_out)]
    attn_b = None
    bias_tables = _bias_table(attn_rel_bias.reshape(-1, attn_rel_bias.shape[-1]))
    for i in range(depth):
        j = i // 2
        g = mix_norm[i][None, :]
        if i % 2 == 0:
            x2 = _pool_layer(x2, g, pool_w[j].astype(_BF16), pool_b[j][None, :],
                             pool_scale[j][None, :], s)
        else:
            wqkv_b, wo_b = attn_b
            qn = jnp.tile(attn_q_norm[j], N_HEADS)[None, :]
            kn = jnp.tile(attn_k_norm[j], N_HEADS)[None, :]
            q, k, vt = _qkv_layer(x2, g, wqkv_b, wqkv_b[:, 2 * d:].T, qn, kn, b)
            shp = (b, s, d)
            x2 = _attn_layer(x2.reshape(shp), q.reshape(shp), k.reshape(shp), vt,
                             bias_tables, j, wo_b).reshape(b * s, d)
        pending = []
        if i + 1 < depth:
            pending += [(w, i + 1) for w in (ffn_w_gate, ffn_w_val, ffn_w_out)]
            if (i + 1) % 2 == 1:
                pending += [(w, (i + 1) // 2) for w in (attn_wqkv, attn_wo)]
        x2, done = _ffn_layer(x2, s, ffn_norm[i][None, :], ffn_b[0], ffn_b[1], ffn_conv_w[i],
                              ffn_conv_b[i][None, :], ffn_b[2], cast=pending)
        ffn_b, attn_b = done[:3], done[3:]
    return x2.reshape(b, s, d)
```

```python
import functools

import jax
import jax.numpy as jnp
from jax import lax
from jax.experimental import pallas as pl
from jax.experimental.pallas import tpu as pltpu

EPS = 1e-6
CHUNK = 64
LEFT_CHUNKS = 8
N_HEADS = 16
HEAD_DIM = 64
REL_CLIP = 256
POOL_WINDOWS = (2, 4, 8, 16)
CONV_W = 3
NEG_BIG = -1e30
LOG2E = 1.4426950408889634

SUBLANES = 8
LANES = 128
MXU_COLS = 256

ROW_TILE = 512
FFN_ROW_TILE = 1024
POOL_ROW_TILE = 2048
QKV_ROW_TILE = 1024
POOL_HALO = 16
POOL_BLOCK = 128
ATTN_SUB = 128
ATTN_WIN = ATTN_SUB + LEFT_CHUNKS * CHUNK
FFN_COLS = 256
VMEM_LIMIT = 56 * 1024 * 1024

_BF16 = jnp.bfloat16
_F32 = jnp.float32


def _rmsnorm(x, g):
    ms = jnp.mean(x * x, axis=-1, keepdims=True)
    return x * lax.rsqrt(ms + EPS) * g


def _dot(a, b):
    return jnp.dot(a, b, preferred_element_type=_F32)


def _whole(shape):
    zeros = (0,) * len(shape)
    return pl.BlockSpec(shape, lambda *_: zeros, pipeline_mode=pl.Buffered(1))


def _params(semantics):
    return pltpu.CompilerParams(dimension_semantics=semantics,
                                vmem_limit_bytes=VMEM_LIMIT)


def _pool_kernel(tiles_per_seq, x_ref, g_ref, band_ref, w_ref, b_ref, sc_ref, o_ref,
                 hext_ref, hi_ref, lo_ref):
    tm, d = x_ref.shape
    t_in_seq = pl.program_id(0) % tiles_per_seq

    @pl.when(t_in_seq == 0)
    def _():
        hext_ref[tm:, :] = jnp.zeros((POOL_HALO, d), _F32)

    hext_ref[0:POOL_HALO, :] = hext_ref[tm:tm + POOL_HALO, :]
    hext_ref[POOL_HALO:, :] = _rmsnorm(x_ref[...], g_ref[...])
    hext = hext_ref[...]
    hi = hext.astype(_BF16)
    hi_ref[...] = hi
    lo_ref[...] = (hext - hi.astype(_F32)).astype(_BF16)

    gw = d // len(POOL_WINDOWS)
    blk = POOL_BLOCK
    for rb in range(tm // blk):
        rows = slice(rb * blk, (rb + 1) * blk)
        src = slice(rb * blk, (rb + 1) * blk + POOL_HALO)
        for gi in range(len(POOL_WINDOWS)):
            cols = slice(gi * gw, (gi + 1) * gw)
            band = band_ref[gi]
            o_ref[rows, cols] = _dot(band, hi_ref[src, cols]) + _dot(band, lo_ref[src, cols])
    for rb in range(tm // blk):
        rows = slice(rb * blk, (rb + 1) * blk)
        pos = t_in_seq * tm + rb * blk + lax.broadcasted_iota(jnp.int32, (blk, 1), 0)
        for gi, w in enumerate(POOL_WINDOWS):
            cols = slice(gi * gw, (gi + 1) * gw)
            inv_cnt = 1.0 / jnp.minimum(pos + 1, w).astype(_F32)
            h = hext_ref[rb * blk + POOL_HALO:(rb + 1) * blk + POOL_HALO, cols]
            y = _dot((o_ref[rows, cols] * inv_cnt - h).astype(_BF16), w_ref[gi])
            o_ref[rows, cols] = x_ref[rows, cols] + (y + b_ref[:, cols]) * sc_ref[:, cols]


def _pool_bands():
    i = jnp.arange(POOL_BLOCK)[:, None] + POOL_HALO
    k = jnp.arange(POOL_BLOCK + POOL_HALO)[None, :]
    return jnp.stack([jnp.logical_and(k <= i, k > i - w) for w in POOL_WINDOWS]).astype(_BF16)


def _pool_layer(x2, g, w, b, sc, seq):
    t, d = x2.shape
    tm = POOL_ROW_TILE
    gw = d // len(POOL_WINDOWS)
    n_g = len(POOL_WINDOWS)
    row = pl.BlockSpec((tm, d), lambda i: (i, 0))
    vec = _whole((1, d))
    return pl.pallas_call(
        functools.partial(_pool_kernel, seq // tm),
        grid=(t // tm,),
        in_specs=[row, vec, _whole((n_g, POOL_BLOCK, POOL_BLOCK + POOL_HALO)),
                  _whole((n_g, gw, gw)), vec, vec],
        out_specs=row,
        out_shape=jax.ShapeDtypeStruct((t, d), _F32),
        scratch_shapes=[pltpu.VMEM((POOL_HALO + tm, d), _F32),
                        pltpu.VMEM((POOL_HALO + tm, d), _BF16),
                        pltpu.VMEM((POOL_HALO + tm, d), _BF16)],
        compiler_params=_params(("arbitrary",)),
        name="pool_layer",
    )(x2, g, _pool_bands(), w, b, sc)


def _shift_rows(a, prev, k):
    rolled = pltpu.roll(a, k, axis=0)
    prolled = pltpu.roll(prev, k, axis=0)
    rows = lax.broadcasted_iota(jnp.int32, prev.shape, 0)
    head = jnp.where(rows < k, prolled, rolled[:SUBLANES, :])
    return jnp.concatenate([head, rolled[SUBLANES:, :]], axis=0)


def _ffn_kernel(tiles_per_seq, n_cast, *refs):
    x_ref, g_ref, wg_ref, wv_ref, cw_ref, cb_ref, wo_ref = refs[:7]
    cast_in = refs[7:7 + n_cast]
    o_ref = refs[7 + n_cast]
    cast_out = refs[8 + n_cast:8 + 2 * n_cast]
    carry_ref, u_ref = refs[8 + 2 * n_cast:]

    @pl.when(pl.program_id(0) % tiles_per_seq == 0)
    def _():
        carry_ref[...] = jnp.zeros_like(carry_ref)

    x = x_ref[...]
    tm = x.shape[0]
    hb = _rmsnorm(x, g_ref[...]).astype(_BF16)
    f = wg_ref.shape[1]
    for c in range(f // FFN_COLS):
        cols = slice(c * FFN_COLS, (c + 1) * FFN_COLS)
        a = _dot(hb, wg_ref[:, cols])
        v = _dot(hb, wv_ref[:, cols])
        prev = carry_ref[:, cols]
        carry_ref[:, cols] = a[tm - SUBLANES:, :]
        a1 = _shift_rows(a, prev, 1)
        a2 = _shift_rows(a, prev, 2)
        z = (a2 * cw_ref[0:1, cols] + a1 * cw_ref[1:2, cols] + a * cw_ref[2:3, cols]
             + cb_ref[:, cols])
        u_ref[:, cols] = (z * jax.nn.sigmoid(z) * v).astype(_BF16)
    o_ref[...] = x + _dot(u_ref[...], wo_ref[...])
    for src, dst in zip(cast_in, cast_out):
        dst[...] = src[...].astype(_BF16)


def _slab_rows(rows, n_steps):
    for n_slabs in range(n_steps, 0, -1):
        if n_steps % n_slabs == 0 and rows % n_slabs == 0 and (rows // n_slabs) % (2 * SUBLANES) == 0:
            return rows // n_slabs
    raise ValueError(f"no slab size for {rows} rows in {n_steps} steps")


def _ffn_layer(x2, seq, g, wg, wv, cw, cb, wo, cast=()):
    t, d = x2.shape
    f = wg.shape[1]
    tm = FFN_ROW_TILE
    n_steps = t // tm
    row = pl.BlockSpec((tm, d), lambda i: (i, 0))
    slabs_in, slabs_out, shapes = [], [], []
    for w, layer in cast:
        _, rows, cols = w.shape
        r = _slab_rows(rows, n_steps)
        hold = n_steps // (rows // r)
        slabs_in.append(pl.BlockSpec((None, r, cols),
                                     lambda i, layer=layer, hold=hold: (layer, i // hold, 0)))
        slabs_out.append(pl.BlockSpec((r, cols), lambda i, hold=hold: (i // hold, 0)))
        shapes.append(jax.ShapeDtypeStruct((rows, cols), _BF16))
    outs = pl.pallas_call(
        functools.partial(_ffn_kernel, seq // tm, len(cast)),
        grid=(n_steps,),
        in_specs=[row, _whole((1, d)), _whole((d, f)), _whole((d, f)),
                  _whole((CONV_W, f)), _whole((1, f)), _whole((f, d))] + slabs_in,
        out_specs=[row] + slabs_out,
        out_shape=[jax.ShapeDtypeStruct((t, d), _F32)] + shapes,
        scratch_shapes=[pltpu.VMEM((SUBLANES, f), _F32), pltpu.VMEM((tm, f), _BF16)],
        compiler_params=_params(("arbitrary",)),
        name="ffn_layer",
    )(x2, g, wg, wv, cw, cb, wo, *[w for w, _ in cast])
    return outs[0], list(outs[1:])


def _head_rms_scale(y):
    low = lax.broadcasted_iota(jnp.int32, (1, LANES), 1) < HEAD_DIM
    sq = y * y
    parts = []
    for j in range(y.shape[1] // LANES):
        blk = sq[:, j * LANES:(j + 1) * LANES]
        s_lo = jnp.sum(jnp.where(low, blk, 0.0), axis=-1, keepdims=True)
        s_hi = jnp.sum(jnp.where(low, 0.0, blk), axis=-1, keepdims=True)
        r_lo = lax.rsqrt(s_lo * (1.0 / HEAD_DIM) + EPS)
        r_hi = lax.rsqrt(s_hi * (1.0 / HEAD_DIM) + EPS)
        parts.append(jnp.where(low, r_lo, r_hi))
    return jnp.concatenate(parts, axis=1)


def _qkv_kernel(x_ref, g_ref, w_ref, wqt_ref, wvt_ref, qn_ref, kn_ref, qt_ref, k_ref, vt_ref):
    x = x_ref[...]
    d = x.shape[1]
    hb = _rmsnorm(x, g_ref[...]).astype(_BF16)
    qt = lax.dot_general(wqt_ref[...], hb, (((1,), (1,)), ((), ())),
                         preferred_element_type=_F32)
    tm = qt.shape[1]
    ms = jnp.mean((qt * qt).reshape(N_HEADS, HEAD_DIM, tm), axis=1, keepdims=True)
    scale = jnp.broadcast_to(lax.rsqrt(ms + EPS), (N_HEADS, HEAD_DIM, tm)).reshape(d, tm)
    qt_ref[...] = (qt * scale * qn_ref[...] * (HEAD_DIM ** -0.5 * LOG2E)).astype(_BF16)
    k = _dot(hb, w_ref[:, d:2 * d])
    k = k * _head_rms_scale(k) * kn_ref[...]
    k_ref[...] = k.astype(_BF16)
    vt_ref[...] = lax.dot_general(wvt_ref[...], hb, (((1,), (1,)), ((), ())),
                                  preferred_element_type=_F32).astype(_BF16)


def _qkv_layer(x2, g, w, wqt, wvt, qn, kn, batch):
    t, d = x2.shape
    tm = QKV_ROW_TILE
    tiles_per_seq = t // batch // tm
    row = pl.BlockSpec((tm, d), lambda i: (i, 0))
    col = pl.BlockSpec((None, d, tm), lambda i: (i // tiles_per_seq, 0, i % tiles_per_seq))
    vec = _whole((1, d))
    out = jax.ShapeDtypeStruct((t, d), _BF16)
    return pl.pallas_call(
        _qkv_kernel,
        grid=(t // tm,),
        in_specs=[row, vec, _whole((d, 2 * d)), _whole((d, d)), _whole((d, d)),
                  _whole((d, 1)), vec],
        out_specs=[col, row, col],
        out_shape=[jax.ShapeDtypeStruct((batch, d, t // batch), _BF16), out,
                   jax.ShapeDtypeStruct((batch, d, t // batch), _BF16)],
        compiler_params=_params(("arbitrary",)),
        name="qkv_layer",
    )(x2, g, w, wqt, wvt, qn, kn)


def _attn_window(s, use_prev):
    n_cur = (s + 1) * ATTN_SUB
    n_prev = ATTN_WIN - n_cur
    return (0 if use_prev else n_prev), n_prev, n_cur


def _attn_scores(s, p, use_prev, q_ref, kp_ref, kc_ref, bt_ref, sc_ref):
    tq = kc_ref.shape[0]
    j0, n_prev, n_cur = _attn_window(s, use_prev)
    cols = slice(p * LANES, (p + 1) * LANES)
    low_rows = lax.broadcasted_iota(jnp.int32, (LANES, ATTN_SUB), 0) < HEAD_DIM
    q2 = q_ref[cols, s * ATTN_SUB:(s + 1) * ATTN_SUB]
    zero = jnp.zeros_like(q2)
    qq = jnp.concatenate([jnp.where(low_rows, q2, zero),
                          jnp.where(low_rows, zero, q2)], axis=1)
    parts = []
    if use_prev:
        parts.append(_dot(kp_ref[tq - n_prev:tq, cols], qq))
    parts.append(_dot(kc_ref[0:n_cur, cols], qq))
    sc_ref[s % 2, p, j0:, :] = jnp.concatenate(parts, axis=0) + bt_ref[p, j0:, :]


def _attn_values(s, p, use_prev, vtp_ref, vtc_ref, sc_ref, att_ref):
    tq = att_ref.shape[0]
    j0, n_prev, n_cur = _attn_window(s, use_prev)
    cols = slice(p * LANES, (p + 1) * LANES)
    low_rows = lax.broadcasted_iota(jnp.int32, (LANES, ATTN_SUB), 0) < HEAD_DIM
    sc = sc_ref[s % 2, p, j0:, :]
    e = jnp.exp2(sc - jnp.max(sc, axis=0, keepdims=True)).astype(_BF16)
    ones = jnp.ones((2 * SUBLANES, n_cur), _BF16)
    ot = _dot(jnp.concatenate([vtc_ref[cols, 0:n_cur], ones], axis=0), e[n_prev - j0:, :])
    if use_prev:
        ones = jnp.ones((2 * SUBLANES, n_prev), _BF16)
        ot = ot + _dot(jnp.concatenate([vtp_ref[cols, tq - n_prev:tq], ones], axis=0),
                       e[0:n_prev, :])
    inv = 1.0 / ot[LANES:LANES + 1, :]
    out_t = jnp.where(low_rows, ot[0:LANES, 0:ATTN_SUB] * inv[:, 0:ATTN_SUB],
                      ot[0:LANES, ATTN_SUB:] * inv[:, ATTN_SUB:])
    att_ref[s * ATTN_SUB:(s + 1) * ATTN_SUB, cols] = out_t.T.astype(_BF16)


def _attn_kernel(x_ref, q_ref, kp_ref, kc_ref, vtp_ref, vtc_ref, bt_ref, wo_ref,
                 o_ref, att_ref, sc_ref):
    first = pl.program_id(1) == 0
    d, tq = q_ref.shape
    n_sub = tq // ATTN_SUB
    n_pairs = d // LANES

    def run(use_prev):
        scores = functools.partial(_attn_scores, use_prev=use_prev, q_ref=q_ref,
                                   kp_ref=kp_ref, kc_ref=kc_ref, bt_ref=bt_ref,
                                   sc_ref=sc_ref)
        values = functools.partial(_attn_values, use_prev=use_prev, vtp_ref=vtp_ref,
                                   vtc_ref=vtc_ref, sc_ref=sc_ref, att_ref=att_ref)
        for p in range(n_pairs):
            scores(0, p)
        for s in range(n_sub):
            for p in range(n_pairs):
                if s + 1 < n_sub:
                    scores(s + 1, p)
                values(s, p)

    pl.when(first)(functools.partial(run, False))
    pl.when(jnp.logical_not(first))(functools.partial(run, True))
    o_ref[...] = x_ref[...] + _dot(att_ref[...], wo_ref[...])


def _attn_layer(x3, q3, k3, vt3, bt, layer, wo):
    b, s, d = x3.shape
    tq = ROW_TILE
    assert tq == LEFT_CHUNKS * CHUNK
    cur = pl.BlockSpec((None, tq, d), lambda bi, i: (bi, i, 0))
    prev = pl.BlockSpec((None, tq, d), lambda bi, i: (bi, jnp.maximum(i - 1, 0), 0))
    cur_t = pl.BlockSpec((None, d, tq), lambda bi, i: (bi, 0, i))
    prev_t = pl.BlockSpec((None, d, tq), lambda bi, i: (bi, 0, jnp.maximum(i - 1, 0)))
    n_pairs = d // LANES
    table = pl.BlockSpec((n_pairs, ATTN_WIN, 2 * ATTN_SUB), lambda bi, i: (layer, 0, 0),
                         pipeline_mode=pl.Buffered(1))
    return pl.pallas_call(
        _attn_kernel,
        grid=(b, s // tq),
        in_specs=[cur, cur_t, prev, cur, prev_t, cur_t, table, _whole((d, d))],
        out_specs=cur,
        out_shape=jax.ShapeDtypeStruct((b, s, d), _F32),
        scratch_shapes=[pltpu.VMEM((tq, d), _BF16),
                        pltpu.VMEM((2, n_pairs, ATTN_WIN, 2 * ATTN_SUB), _F32)],
        compiler_params=_params(("arbitrary", "arbitrary")),
        name="attn_layer",
    )(x3, q3, k3, k3, vt3, vt3, bt, wo)


BIAS_ROW = 768


def _bias_kernel(c_ref, o_ref):
    kc = lax.broadcasted_iota(jnp.int32, (ATTN_WIN, ATTN_SUB), 0) // CHUNK
    qc = lax.broadcasted_iota(jnp.int32, (ATTN_WIN, ATTN_SUB), 1) // CHUNK
    valid = jnp.logical_and(kc >= qc, kc <= qc + LEFT_CHUNKS)
    halves = []
    for hh in range(2):
        x = jnp.broadcast_to(c_ref[hh], (ATTN_WIN, BIAS_ROW))
        y = pltpu.roll(x, BIAS_ROW - ATTN_WIN + 1, axis=1, stride=1, stride_axis=0)
        halves.append(jnp.where(valid, y[:, :ATTN_SUB] * LOG2E, NEG_BIG))
    o_ref[0] = jnp.concatenate(halves, axis=1)


def _bias_table(rel_table):
    h, n = rel_table.shape
    lo = n - 1 - REL_CLIP - (ATTN_SUB - 1)
    tail = jnp.broadcast_to(rel_table[:, n - 1:n], (h, BIAS_ROW - (n - lo)))
    c = jnp.concatenate([rel_table[:, lo:], tail], axis=1).astype(_F32)[:, None, :]
    return pl.pallas_call(
        _bias_kernel,
        grid=(h // 2,),
        in_specs=[pl.BlockSpec((2, 1, BIAS_ROW), lambda i: (i, 0, 0))],
        out_specs=pl.BlockSpec((1, ATTN_WIN, 2 * ATTN_SUB), lambda i: (i, 0, 0)),
        out_shape=jax.ShapeDtypeStruct((h // 2, ATTN_WIN, 2 * ATTN_SUB), _F32),
        compiler_params=_params(("arbitrary",)),
        name="bias_table",
    )(c)


def kernel(x, mix_norm, ffn_norm, pool_w, pool_b, pool_scale, attn_wqkv, attn_q_norm,
           attn_k_norm, attn_rel_bias, attn_wo, ffn_w_gate, ffn_w_val, ffn_conv_w,
           ffn_conv_b, ffn_w_out):
    b, s, d = x.shape
    depth = mix_norm.shape[0]
    assert all(s % tile == 0
               for tile in (ROW_TILE, FFN_ROW_TILE, POOL_ROW_TILE, QKV_ROW_TILE))
    assert d % MXU_COLS == 0 and d == N_HEADS * HEAD_DIM
    assert ffn_w_gate.shape[2] % FFN_COLS == 0
    x2 = x.reshape(b * s, d)
    ffn_b = [w[0].astype(_BF16) for w in (ffn_w_gate, ffn_w_val, ffn_w_out)]
    attn_b = None
    bias_tables = _bias_table(attn_rel_bias.reshape(-1, attn_rel_bias.shape[-1]))
    for i in range(depth):
        j = i // 2
        g = mix_norm[i][None, :]
        if i % 2 == 0:
            x2 = _pool_layer(x2, g, pool_w[j].astype(_BF16), pool_b[j][None, :],
                             pool_scale[j][None, :], s)
        else:
            wqkv_b, wo_b = attn_b
            qn = jnp.tile(attn_q_norm[j], N_HEADS)[:, None]
            kn = jnp.tile(attn_k_norm[j], N_HEADS)[None, :]
            qt, k, vt = _qkv_layer(x2, g, wqkv_b, wqkv_b[:, :d].T, wqkv_b[:, 2 * d:].T, qn,
                                   kn, b)
            shp = (b, s, d)
            x2 = _attn_layer(x2.reshape(shp), qt, k.reshape(shp), vt,
                             bias_tables, j, wo_b).reshape(b * s, d)
        pending = []
        if i + 1 < depth:
            pending += [(w, i + 1) for w in (ffn_w_gate, ffn_w_val, ffn_w_out)]
            if (i + 1) % 2 == 1:
                pending += [(w, (i + 1) // 2) for w in (attn_wqkv, attn_wo)]
        x2, done = _ffn_layer(x2, s, ffn_norm[i][None, :], ffn_b[0], ffn_b[1], ffn_conv_w[i],
                              ffn_conv_b[i][None, :], ffn_b[2], cast=pending)
        ffn_b, attn_b = done[:3], done[3:]
    return x2.reshape(b, s, d)
```

```python
import functools

import jax
import jax.numpy as jnp
from jax import lax
from jax.experimental import pallas as pl
from jax.experimental.pallas import tpu as pltpu

EPS = 1e-6
CHUNK = 64
LEFT_CHUNKS = 8
N_HEADS = 16
HEAD_DIM = 64
REL_CLIP = 256
POOL_WINDOWS = (2, 4, 8, 16)
CONV_W = 3
NEG_BIG = -1e30
LOG2E = 1.4426950408889634

SUBLANES = 8
LANES = 128
MXU_COLS = 256

ROW_TILE = 512
FFN_ROW_TILE = 1024
POOL_ROW_TILE = 2048
QKV_ROW_TILE = 1024
POOL_HALO = 16
POOL_BLOCK = 128
ATTN_SUB = 128
ATTN_WIN = ATTN_SUB + LEFT_CHUNKS * CHUNK
FFN_COLS = 256
FFN_ROW_SLABS = 4
VMEM_LIMIT = 56 * 1024 * 1024

_BF16 = jnp.bfloat16
_F32 = jnp.float32


def _rmsnorm(x, g):
    ms = jnp.mean(x * x, axis=-1, keepdims=True)
    return x * lax.rsqrt(ms + EPS) * g


def _dot(a, b):
    return jnp.dot(a, b, preferred_element_type=_F32)


def _whole(shape):
    zeros = (0,) * len(shape)
    return pl.BlockSpec(shape, lambda *_: zeros, pipeline_mode=pl.Buffered(1))


def _params(semantics):
    return pltpu.CompilerParams(dimension_semantics=semantics,
                                vmem_limit_bytes=VMEM_LIMIT)


def _pool_kernel(tiles_per_seq, x_ref, g_ref, band_ref, w_ref, b_ref, sc_ref, o_ref,
                 hext_ref, hi_ref, lo_ref):
    tm, d = x_ref.shape
    t_in_seq = pl.program_id(0) % tiles_per_seq

    @pl.when(t_in_seq == 0)
    def _():
        hext_ref[tm:, :] = jnp.zeros((POOL_HALO, d), _F32)

    hext_ref[0:POOL_HALO, :] = hext_ref[tm:tm + POOL_HALO, :]
    hext_ref[POOL_HALO:, :] = _rmsnorm(x_ref[...], g_ref[...])
    hext = hext_ref[...]
    hi = hext.astype(_BF16)
    hi_ref[...] = hi
    lo_ref[...] = (hext - hi.astype(_F32)).astype(_BF16)

    gw = d // len(POOL_WINDOWS)
    blk = POOL_BLOCK
    for rb in range(tm // blk):
        rows = slice(rb * blk, (rb + 1) * blk)
        src = slice(rb * blk, (rb + 1) * blk + POOL_HALO)
        for gi in range(len(POOL_WINDOWS)):
            cols = slice(gi * gw, (gi + 1) * gw)
            band = band_ref[gi]
            o_ref[rows, cols] = _dot(band, hi_ref[src, cols]) + _dot(band, lo_ref[src, cols])
    for rb in range(tm // blk):
        rows = slice(rb * blk, (rb + 1) * blk)
        pos = t_in_seq * tm + rb * blk + lax.broadcasted_iota(jnp.int32, (blk, 1), 0)
        for gi, w in enumerate(POOL_WINDOWS):
            cols = slice(gi * gw, (gi + 1) * gw)
            inv_cnt = 1.0 / jnp.minimum(pos + 1, w).astype(_F32)
            h = hext_ref[rb * blk + POOL_HALO:(rb + 1) * blk + POOL_HALO, cols]
            y = _dot((o_ref[rows, cols] * inv_cnt - h).astype(_BF16), w_ref[gi])
            o_ref[rows, cols] = x_ref[rows, cols] + (y + b_ref[:, cols]) * sc_ref[:, cols]


def _pool_bands():
    i = jnp.arange(POOL_BLOCK)[:, None] + POOL_HALO
    k = jnp.arange(POOL_BLOCK + POOL_HALO)[None, :]
    return jnp.stack([jnp.logical_and(k <= i, k > i - w) for w in POOL_WINDOWS]).astype(_BF16)


def _pool_layer(x2, g, w, b, sc, seq):
    t, d = x2.shape
    tm = POOL_ROW_TILE
    gw = d // len(POOL_WINDOWS)
    n_g = len(POOL_WINDOWS)
    row = pl.BlockSpec((tm, d), lambda i: (i, 0))
    vec = _whole((1, d))
    return pl.pallas_call(
        functools.partial(_pool_kernel, seq // tm),
        grid=(t // tm,),
        in_specs=[row, vec, _whole((n_g, POOL_BLOCK, POOL_BLOCK + POOL_HALO)),
                  _whole((n_g, gw, gw)), vec, vec],
        out_specs=row,
        out_shape=jax.ShapeDtypeStruct((t, d), _F32),
        scratch_shapes=[pltpu.VMEM((POOL_HALO + tm, d), _F32),
                        pltpu.VMEM((POOL_HALO + tm, d), _BF16),
                        pltpu.VMEM((POOL_HALO + tm, d), _BF16)],
        compiler_params=_params(("arbitrary",)),
        name="pool_layer",
    )(x2, g, _pool_bands(), w, b, sc)


def _shift_rows(a, prev, k):
    rolled = pltpu.roll(a, k, axis=0)
    prolled = pltpu.roll(prev, k, axis=0)
    rows = lax.broadcasted_iota(jnp.int32, prev.shape, 0)
    head = jnp.where(rows < k, prolled, rolled[:SUBLANES, :])
    return jnp.concatenate([head, rolled[SUBLANES:, :]], axis=0)


def _ffn_kernel(tiles_per_seq, n_cast, *refs):
    x_ref, g_ref, wg_ref, wv_ref, cw_ref, cb_ref, wo_ref = refs[:7]
    cast_in = refs[7:7 + n_cast]
    o_ref = refs[7 + n_cast]
    cast_out = refs[8 + n_cast:8 + 2 * n_cast]
    carry_ref, u_ref = refs[8 + 2 * n_cast:]

    @pl.when(pl.program_id(0) % tiles_per_seq == 0)
    def _():
        carry_ref[...] = jnp.zeros_like(carry_ref)

    x = x_ref[...]
    tm = x.shape[0]
    hb = _rmsnorm(x, g_ref[...]).astype(_BF16)
    f = wg_ref.shape[1]
    for c in range(f // FFN_COLS):
        cols = slice(c * FFN_COLS, (c + 1) * FFN_COLS)
        a = _dot(hb, wg_ref[:, cols])
        v = _dot(hb, wv_ref[:, cols])
        prev = carry_ref[:, cols]
        carry_ref[:, cols] = a[tm - SUBLANES:, :]
        slab = tm // FFN_ROW_SLABS
        for r in range(FFN_ROW_SLABS):
            rows = slice(r * slab, (r + 1) * slab)
            ar, vr = a[rows, :], v[rows, :]
            a1 = _shift_rows(ar, prev, 1)
            a2 = _shift_rows(ar, prev, 2)
            z = (a2 * cw_ref[0:1, cols] + a1 * cw_ref[1:2, cols] + ar * cw_ref[2:3, cols]
                 + cb_ref[:, cols])
            u_ref[rows, cols] = (z * jax.nn.sigmoid(z) * vr).astype(_BF16)
            prev = ar[slab - SUBLANES:, :]
    o_ref[...] = x + _dot(u_ref[...], wo_ref[...])
    for src, dst in zip(cast_in, cast_out):
        dst[...] = src[...].astype(_BF16)


def _slab_rows(rows, n_steps):
    for n_slabs in range(n_steps, 0, -1):
        if n_steps % n_slabs == 0 and rows % n_slabs == 0 and (rows // n_slabs) % (2 * SUBLANES) == 0:
            return rows // n_slabs
    raise ValueError(f"no slab size for {rows} rows in {n_steps} steps")


def _ffn_layer(x2, seq, g, wg, wv, cw, cb, wo, cast=()):
    t, d = x2.shape
    f = wg.shape[1]
    tm = FFN_ROW_TILE
    n_steps = t // tm
    row = pl.BlockSpec((tm, d), lambda i: (i, 0))
    slabs_in, slabs_out, shapes = [], [], []
    for w, layer in cast:
        _, rows, cols = w.shape
        r = _slab_rows(rows, n_steps)
        hold = n_steps // (rows // r)
        slabs_in.append(pl.BlockSpec((None, r, cols),
                                     lambda i, layer=layer, hold=hold: (layer, i // hold, 0)))
        slabs_out.append(pl.BlockSpec((r, cols), lambda i, hold=hold: (i // hold, 0)))
        shapes.append(jax.ShapeDtypeStruct((rows, cols), _BF16))
    outs = pl.pallas_call(
        functools.partial(_ffn_kernel, seq // tm, len(cast)),
        grid=(n_steps,),
        in_specs=[row, _whole((1, d)), _whole((d, f)), _whole((d, f)),
                  _whole((CONV_W, f)), _whole((1, f)), _whole((f, d))] + slabs_in,
        out_specs=[row] + slabs_out,
        out_shape=[jax.ShapeDtypeStruct((t, d), _F32)] + shapes,
        scratch_shapes=[pltpu.VMEM((SUBLANES, f), _F32), pltpu.VMEM((tm, f), _BF16)],
        compiler_params=_params(("arbitrary",)),
        name="ffn_layer",
    )(x2, g, wg, wv, cw, cb, wo, *[w for w, _ in cast])
    return outs[0], list(outs[1:])


def _head_rms_scale(y):
    low = lax.broadcasted_iota(jnp.int32, (1, LANES), 1) < HEAD_DIM
    sq = y * y
    parts = []
    for j in range(y.shape[1] // LANES):
        blk = sq[:, j * LANES:(j + 1) * LANES]
        s_lo = jnp.sum(jnp.where(low, blk, 0.0), axis=-1, keepdims=True)
        s_hi = jnp.sum(jnp.where(low, 0.0, blk), axis=-1, keepdims=True)
        r_lo = lax.rsqrt(s_lo * (1.0 / HEAD_DIM) + EPS)
        r_hi = lax.rsqrt(s_hi * (1.0 / HEAD_DIM) + EPS)
        parts.append(jnp.where(low, r_lo, r_hi))
    return jnp.concatenate(parts, axis=1)


def _qkv_kernel(x_ref, g_ref, w_ref, wvt_ref, qn_ref, kn_ref, q_ref, k_ref, vt_ref):
    x = x_ref[...]
    d = x.shape[1]
    hb = _rmsnorm(x, g_ref[...]).astype(_BF16)
    q = _dot(hb, w_ref[:, 0:d])
    q = q * _head_rms_scale(q) * qn_ref[...]
    q_ref[...] = (q * (HEAD_DIM ** -0.5 * LOG2E)).astype(_BF16)
    k = _dot(hb, w_ref[:, d:2 * d])
    k = k * _head_rms_scale(k) * kn_ref[...]
    k_ref[...] = k.astype(_BF16)
    vt_ref[...] = lax.dot_general(wvt_ref[...], hb, (((1,), (1,)), ((), ())),
                                  preferred_element_type=_F32).astype(_BF16)


def _qkv_layer(x2, g, w, wvt, qn, kn, batch):
    t, d = x2.shape
    tm = QKV_ROW_TILE
    tiles_per_seq = t // batch // tm
    row = pl.BlockSpec((tm, d), lambda i: (i, 0))
    col = pl.BlockSpec((None, d, tm), lambda i: (i // tiles_per_seq, 0, i % tiles_per_seq))
    vec = _whole((1, d))
    out = jax.ShapeDtypeStruct((t, d), _BF16)
    return pl.pallas_call(
        _qkv_kernel,
        grid=(t // tm,),
        in_specs=[row, vec, _whole((d, 2 * d)), _whole((d, d)), vec, vec],
        out_specs=[row, row, col],
        out_shape=[out, out, jax.ShapeDtypeStruct((batch, d, t // batch), _BF16)],
        compiler_params=_params(("arbitrary",)),
        name="qkv_layer",
    )(x2, g, w, wvt, qn, kn)


def _attn_window(s, use_prev):
    n_cur = (s + 1) * ATTN_SUB
    n_prev = ATTN_WIN - n_cur
    return (0 if use_prev else n_prev), n_prev, n_cur


def _attn_scores(s, p, use_prev, q_ref, kp_ref, kc_ref, bt_ref, sc_ref):
    tq = q_ref.shape[0]
    j0, n_prev, n_cur = _attn_window(s, use_prev)
    cols = slice(p * LANES, (p + 1) * LANES)
    low_lanes = lax.broadcasted_iota(jnp.int32, (ATTN_SUB, LANES), 1) < HEAD_DIM
    q2 = q_ref[s * ATTN_SUB:(s + 1) * ATTN_SUB, cols]
    zero = jnp.zeros_like(q2)
    qq = jnp.concatenate([jnp.where(low_lanes, q2, zero),
                          jnp.where(low_lanes, zero, q2)], axis=0)
    trans_b = (((1,), (1,)), ((), ()))
    parts = []
    if use_prev:
        parts.append(lax.dot_general(kp_ref[tq - n_prev:tq, cols], qq, trans_b,
                                     preferred_element_type=_F32))
    parts.append(lax.dot_general(kc_ref[0:n_cur, cols], qq, trans_b,
                                 preferred_element_type=_F32))
    sc_ref[s % 2, p, j0:, :] = jnp.concatenate(parts, axis=0) + bt_ref[p, j0:, :]


def _attn_values(s, p, use_prev, vtp_ref, vtc_ref, sc_ref, att_ref):
    tq = att_ref.shape[0]
    j0, n_prev, n_cur = _attn_window(s, use_prev)
    cols = slice(p * LANES, (p + 1) * LANES)
    low_rows = lax.broadcasted_iota(jnp.int32, (LANES, ATTN_SUB), 0) < HEAD_DIM
    sc = sc_ref[s % 2, p, j0:, :]
    e = jnp.exp2(sc - jnp.max(sc, axis=0, keepdims=True)).astype(_BF16)
    ones = jnp.ones((2 * SUBLANES, n_cur), _BF16)
    ot = _dot(jnp.concatenate([vtc_ref[cols, 0:n_cur], ones], axis=0), e[n_prev - j0:, :])
    if use_prev:
        ones = jnp.ones((2 * SUBLANES, n_prev), _BF16)
        ot = ot + _dot(jnp.concatenate([vtp_ref[cols, tq - n_prev:tq], ones], axis=0),
                       e[0:n_prev, :])
    inv = 1.0 / ot[LANES:LANES + 1, :]
    out_t = jnp.where(low_rows, ot[0:LANES, 0:ATTN_SUB] * inv[:, 0:ATTN_SUB],
                      ot[0:LANES, ATTN_SUB:] * inv[:, ATTN_SUB:])
    att_ref[s * ATTN_SUB:(s + 1) * ATTN_SUB, cols] = out_t.T.astype(_BF16)


def _attn_kernel(x_ref, q_ref, kp_ref, kc_ref, vtp_ref, vtc_ref, bt_ref, wo_ref,
                 o_ref, att_ref, sc_ref):
    first = pl.program_id(1) == 0
    tq, d = q_ref.shape
    n_sub = tq // ATTN_SUB
    n_pairs = d // LANES

    def run(use_prev):
        scores = functools.partial(_attn_scores, use_prev=use_prev, q_ref=q_ref,
                                   kp_ref=kp_ref, kc_ref=kc_ref, bt_ref=bt_ref,
                                   sc_ref=sc_ref)
        values = functools.partial(_attn_values, use_prev=use_prev, vtp_ref=vtp_ref,
                                   vtc_ref=vtc_ref, sc_ref=sc_ref, att_ref=att_ref)
        for p in range(n_pairs):
            scores(0, p)
        for s in range(n_sub):
            for p in range(n_pairs):
                if s + 1 < n_sub:
                    scores(s + 1, p)
                values(s, p)

    pl.when(first)(functools.partial(run, False))
    pl.when(jnp.logical_not(first))(functools.partial(run, True))
    o_ref[...] = x_ref[...] + _dot(att_ref[...], wo_ref[...])


def _attn_layer(x3, q3, k3, vt3, bt, layer, wo):
    b, s, d = x3.shape
    tq = ROW_TILE
    assert tq == LEFT_CHUNKS * CHUNK
    cur = pl.BlockSpec((None, tq, d), lambda bi, i: (bi, i, 0))
    prev = pl.BlockSpec((None, tq, d), lambda bi, i: (bi, jnp.maximum(i - 1, 0), 0))
    cur_t = pl.BlockSpec((None, d, tq), lambda bi, i: (bi, 0, i))
    prev_t = pl.BlockSpec((None, d, tq), lambda bi, i: (bi, 0, jnp.maximum(i - 1, 0)))
    n_pairs = d // LANES
    table = pl.BlockSpec((n_pairs, ATTN_WIN, 2 * ATTN_SUB), lambda bi, i: (layer, 0, 0),
                         pipeline_mode=pl.Buffered(1))
    return pl.pallas_call(
        _attn_kernel,
        grid=(b, s // tq),
        in_specs=[cur, cur, prev, cur, prev_t, cur_t, table, _whole((d, d))],
        out_specs=cur,
        out_shape=jax.ShapeDtypeStruct((b, s, d), _F32),
        scratch_shapes=[pltpu.VMEM((tq, d), _BF16),
                        pltpu.VMEM((2, n_pairs, ATTN_WIN, 2 * ATTN_SUB), _F32)],
        compiler_params=_params(("arbitrary", "arbitrary")),
        name="attn_layer",
    )(x3, q3, k3, k3, vt3, vt3, bt, wo)


BIAS_ROW = 768


def _bias_kernel(c_ref, o_ref):
    kc = lax.broadcasted_iota(jnp.int32, (ATTN_WIN, ATTN_SUB), 0) // CHUNK
    qc = lax.broadcasted_iota(jnp.int32, (ATTN_WIN, ATTN_SUB), 1) // CHUNK
    valid = jnp.logical_and(kc >= qc, kc <= qc + LEFT_CHUNKS)
    halves = []
    for hh in range(2):
        x = jnp.broadcast_to(c_ref[hh], (ATTN_WIN, BIAS_ROW))
        y = pltpu.roll(x, BIAS_ROW - ATTN_WIN + 1, axis=1, stride=1, stride_axis=0)
        halves.append(jnp.where(valid, y[:, :ATTN_SUB] * LOG2E, NEG_BIG))
    o_ref[0] = jnp.concatenate(halves, axis=1)


def _bias_table(rel_table):
    h, n = rel_table.shape
    lo = n - 1 - REL_CLIP - (ATTN_SUB - 1)
    tail = jnp.broadcast_to(rel_table[:, n - 1:n], (h, BIAS_ROW - (n - lo)))
    c = jnp.concatenate([rel_table[:, lo:], tail], axis=1).astype(_F32)[:, None, :]
    return pl.pallas_call(
        _bias_kernel,
        grid=(h // 2,),
        in_specs=[pl.BlockSpec((2, 1, BIAS_ROW), lambda i: (i, 0, 0))],
        out_specs=pl.BlockSpec((1, ATTN_WIN, 2 * ATTN_SUB), lambda i: (i, 0, 0)),
        out_shape=jax.ShapeDtypeStruct((h // 2, ATTN_WIN, 2 * ATTN_SUB), _F32),
        compiler_params=_params(("arbitrary",)),
        name="bias_table",
    )(c)


def kernel(x, mix_norm, ffn_norm, pool_w, pool_b, pool_scale, attn_wqkv, attn_q_norm,
           attn_k_norm, attn_rel_bias, attn_wo, ffn_w_gate, ffn_w_val, ffn_conv_w,
           ffn_conv_b, ffn_w_out):
    b, s, d = x.shape
    depth = mix_norm.shape[0]
    assert all(s % tile == 0
               for tile in (ROW_TILE, FFN_ROW_TILE, POOL_ROW_TILE, QKV_ROW_TILE))
    assert d % MXU_COLS == 0 and d == N_HEADS * HEAD_DIM
    assert ffn_w_gate.shape[2] % FFN_COLS == 0
    x2 = x.reshape(b * s, d)
    ffn_b = [w[0].astype(_BF16) for w in (ffn_w_gate, ffn_w_val, ffn_w_out)]
    attn_b = None
    bias_tables = _bias_table(attn_rel_bias.reshape(-1, attn_rel_bias.shape[-1]))
    for i in range(depth):
        j = i // 2
        g = mix_norm[i][None, :]
        if i % 2 == 0:
            x2 = _pool_layer(x2, g, pool_w[j].astype(_BF16), pool_b[j][None, :],
                             pool_scale[j][None, :], s)
        else:
            wqkv_b, wo_b = attn_b
            qn = jnp.tile(attn_q_norm[j], N_HEADS)[None, :]
            kn = jnp.tile(attn_k_norm[j], N_HEADS)[None, :]
            q, k, vt = _qkv_layer(x2, g, wqkv_b, wqkv_b[:, 2 * d:].T, qn, kn, b)
            shp = (b, s, d)
            x2 = _attn_layer(x2.reshape(shp), q.reshape(shp), k.reshape(shp), vt,
                             bias_tables, j, wo_b).reshape(b * s, d)
        pending = []
        if i + 1 < depth:
            pending += [(w, i + 1) for w in (ffn_w_gate, ffn_w_val, ffn_w_out)]
            if (i + 1) % 2 == 1:
                pending += [(w, (i + 1) // 2) for w in (attn_wqkv, attn_wo)]
        x2, done = _ffn_layer(x2, s, ffn_norm[i][None, :], ffn_b[0], ffn_b[1], ffn_conv_w[i],
                              ffn_conv_b[i][None, :], ffn_b[2], cast=pending)
        ffn_b, attn_b = done[:3], done[3:]
    return x2.reshape(b, s, d)
```
